```python
import math
import jax, jax.numpy as jnp
from jax import lax
import numpy as np

D_MODEL = 1024
BATCH = 1
SEQ = 16384
DEPTH = 1

CHUNK = 64
Q_BLOCK = 128
N_HEADS_ATTN = 8
HEAD_DIM = 64
V_HEAD_DIM = 2 * HEAD_DIM
QK_WIDTH = N_HEADS_ATTN * 2 * HEAD_DIM
ATTN_WIDTH = N_HEADS_ATTN * V_HEAD_DIM
POOL_WINDOWS = (2, 4, 8, 16)
N_POOL_GROUPS = len(POOL_WINDOWS)
POOL_WIDTH = D_MODEL
POOL_GROUP_WIDTH = POOL_WIDTH // N_POOL_GROUPS
N_BRANCHES = 2
IN_WIDTH = 2 * QK_WIDTH + ATTN_WIDTH + POOL_WIDTH + N_BRANCHES * D_MODEL
D_FF = 4 * D_MODEL
NUM_BUCKETS = 32
MAX_DISTANCE = 128
NORM_EPS = 1e-6
SUBLN_EPS = 1e-5
NEG_INF = -1e30

kernel_name = "chunk_causal_diffattn_pool_gated_hybrid"


def rms_norm(x, g, eps=NORM_EPS):
    xf = x.astype(jnp.float32)
    y = xf * lax.rsqrt(jnp.mean(xf * xf, axis=-1, keepdims=True) + eps)
    return (y * g.astype(jnp.float32)).astype(x.dtype)


def t5_bucket(rel):
    nb = NUM_BUCKETS // 2
    max_exact = nb // 2
    bucket = (rel > 0).astype(jnp.int32) * nb
    n = jnp.abs(rel)
    n_f = jnp.maximum(n, 1).astype(jnp.float32)
    large = max_exact + (jnp.log(n_f / max_exact) / math.log(MAX_DISTANCE / max_exact)
                         * (nb - max_exact)).astype(jnp.int32)
    large = jnp.minimum(large, nb - 1)
    return bucket + jnp.where(n < max_exact, n, large)


def diff_attention(q, k, v, rel_table, lam):
    B, S = q.shape[0], q.shape[1]
    n_blk = S // Q_BLOCK
    q_blocks = jnp.moveaxis(q.reshape(B, n_blk, Q_BLOCK, N_HEADS_ATTN, 2, HEAD_DIM), 1, 0)
    k_pos = jnp.arange(S, dtype=jnp.int32)
    scale = HEAD_DIM ** -0.5

    def block(args):
        qb, bi = args
        q_pos = bi * Q_BLOCK + jnp.arange(Q_BLOCK, dtype=jnp.int32)
        s = jnp.einsum('bqhcd,bkhcd->bchqk', qb, k).astype(jnp.float32) * scale
        bias = rel_table[t5_bucket(k_pos[None, :] - q_pos[:, None])]
        bias = jnp.moveaxis(bias, -1, 0).astype(jnp.float32)
        visible = (k_pos[None, :] // CHUNK) <= (q_pos[:, None] // CHUNK)
        s = jnp.where(visible, s + bias, NEG_INF)
        p = jax.nn.softmax(s, axis=-1)
        a = (p[:, 0] - lam * p[:, 1]).astype(v.dtype)
        return jnp.einsum('bhqk,bkhe->bqhe', a, v)

    o = lax.map(block, (q_blocks, jnp.arange(n_blk, dtype=jnp.int32)))
    return jnp.moveaxis(o, 0, 1).reshape(B, S, N_HEADS_ATTN, V_HEAD_DIM)


def causal_window_mean(u, w):
    S = u.shape[1]
    c = jnp.cumsum(u.astype(jnp.float32), axis=1)
    c_pad = jnp.pad(c, ((0, 0), (w, 0), (0, 0)))
    window_sum = c_pad[:, w:] - c_pad[:, :S]
    count = jnp.minimum(jnp.arange(S, dtype=jnp.int32) + 1, w).astype(jnp.float32)
    return (window_sum / count[None, :, None]).astype(u.dtype)


def multiscale_pool(u, w_grp, scale):
    B, S = u.shape[0], u.shape[1]
    ug = u.reshape(B, S, N_POOL_GROUPS, POOL_GROUP_WIDTH)
    z = jnp.stack([causal_window_mean(ug[:, :, g], w) - ug[:, :, g]
                   for g, w in enumerate(POOL_WINDOWS)], axis=2)
    z = jnp.einsum('bsgc,gcd->bsgd', z, w_grp).reshape(B, S, POOL_WIDTH)
    return z * scale


def setup_inputs(seed: int = 0) -> dict:
    key = jax.random.key(seed)
    ks = jax.random.split(key, 20)
    f32 = jnp.float32
    nrm = lambda k, shape, s: jax.random.normal(k, shape, f32) * s
    return {
        "x": nrm(ks[0], (BATCH, SEQ, D_MODEL), 1.0),
        "ln_mix_g": 1.0 + nrm(ks[1], (DEPTH, D_MODEL), 0.02),
        "w_in": nrm(ks[2], (DEPTH, D_MODEL, IN_WIDTH), D_MODEL ** -0.5),
        "rel_bias_table": nrm(ks[3], (NUM_BUCKETS, N_HEADS_ATTN), 0.5),
        "lambda_q1": nrm(ks[4], (DEPTH, HEAD_DIM), 0.1),
        "lambda_k1": nrm(ks[5], (DEPTH, HEAD_DIM), 0.1),
        "lambda_q2": nrm(ks[6], (DEPTH, HEAD_DIM), 0.1),
        "lambda_k2": nrm(ks[7], (DEPTH, HEAD_DIM), 0.1),
        "subln_g": 1.0 + nrm(ks[8], (DEPTH, V_HEAD_DIM), 0.02),
        "w_proj_attn": nrm(ks[9], (DEPTH, ATTN_WIDTH, D_MODEL), ATTN_WIDTH ** -0.5),
        "w_pool_grp": nrm(ks[10], (DEPTH, N_POOL_GROUPS, POOL_GROUP_WIDTH, POOL_GROUP_WIDTH), POOL_GROUP_WIDTH ** -0.5),
        "pool_scale": 1.0 + nrm(ks[11], (DEPTH, POOL_WIDTH), 0.02),
        "w_proj_pool": nrm(ks[12], (DEPTH, POOL_WIDTH, D_MODEL), POOL_WIDTH ** -0.5),
        "w_out": nrm(ks[13], (DEPTH, D_MODEL, D_MODEL), D_MODEL ** -0.5),
        "ln_mlp_g": 1.0 + nrm(ks[14], (DEPTH, D_MODEL), 0.02),
        "w_mlp_up": nrm(ks[15], (DEPTH, D_MODEL, D_FF), D_MODEL ** -0.5),
        "w_mlp_down": nrm(ks[16], (DEPTH, D_FF, D_MODEL), D_FF ** -0.5),
        "ln_final_g": 1.0 + nrm(ks[17], (D_MODEL,), 0.02),
    }


def reference(x, ln_mix_g, w_in, rel_bias_table, lambda_q1, lambda_k1, lambda_q2, lambda_k2,
              subln_g, w_proj_attn, w_pool_grp, pool_scale, w_proj_pool, w_out,
              ln_mlp_g, w_mlp_up, w_mlp_down, ln_final_g):
    B, S = x.shape[0], x.shape[1]
    splits = [QK_WIDTH, 2 * QK_WIDTH, 2 * QK_WIDTH + ATTN_WIDTH,
              2 * QK_WIDTH + ATTN_WIDTH + POOL_WIDTH]
    for l in range(DEPTH):
        h = rms_norm(x, ln_mix_g[l])
        proj = h @ w_in[l]
        q, k, v, u, gate_logits = jnp.split(proj, splits, axis=-1)
        q = q.reshape(B, S, N_HEADS_ATTN, 2, HEAD_DIM)
        k = k.reshape(B, S, N_HEADS_ATTN, 2, HEAD_DIM)
        v = v.reshape(B, S, N_HEADS_ATTN, V_HEAD_DIM)

        lambda_init = 0.8 - 0.6 * math.exp(-0.3 * l)
        lam = (jnp.exp(jnp.sum(lambda_q1[l].astype(jnp.float32) * lambda_k1[l].astype(jnp.float32)))
               - jnp.exp(jnp.sum(lambda_q2[l].astype(jnp.float32) * lambda_k2[l].astype(jnp.float32)))
               + lambda_init)
        o = diff_attention(q, k, v, rel_bias_table, lam)
        o = rms_norm(o, subln_g[l], SUBLN_EPS) * (1.0 - lambda_init)
        y_attn = o.reshape(B, S, ATTN_WIDTH) @ w_proj_attn[l]

        y_pool = multiscale_pool(u, w_pool_grp[l], pool_scale[l]) @ w_proj_pool[l]

        g_attn, g_pool = jnp.split(jax.nn.sigmoid(gate_logits), N_BRANCHES, axis=-1)
        x = x + (g_attn * y_attn + g_pool * y_pool) @ w_out[l]

        h = rms_norm(x, ln_mlp_g[l])
        x = x + jnp.square(jax.nn.relu(h @ w_mlp_up[l])) @ w_mlp_down[l]
    return rms_norm(x, ln_final_g)
```

```python
import functools
import math

import jax
import jax.numpy as jnp
from jax import lax
from jax.experimental import pallas as pl
from jax.experimental.pallas import tpu as pltpu

D_MODEL = 1024
N_HEADS = 8
HEAD_DIM = 64
V_HEAD_DIM = 2 * HEAD_DIM
QK_WIDTH = N_HEADS * 2 * HEAD_DIM
ATTN_WIDTH = N_HEADS * V_HEAD_DIM
POOL_WINDOWS = (2, 4, 8, 16)
POOL_GROUP_WIDTH = D_MODEL // len(POOL_WINDOWS)
MAX_WINDOW = max(POOL_WINDOWS)
D_FF = 4 * D_MODEL
CHUNK = 64
NUM_BUCKETS = 32
MAX_DISTANCE = 128
NORM_EPS = 1e-6
SUBLN_EPS = 1e-5
NEG_INF = -1e30
LAMBDA_INIT = 0.8 - 0.6 * math.exp(-0.3 * 0)

SEQ_TILE = 512
FF_CHUNK = 1024
VMEM_LIMIT_BYTES = 56 * 1024 * 1024

_BF16 = jnp.bfloat16
_F32 = jnp.float32


def _rms_norm(x, g, eps):
    return x * lax.rsqrt(jnp.mean(x * x, axis=-1, keepdims=True) + eps) * g


def _resident(shape):
    return pl.BlockSpec(shape, lambda *_: (0,) * len(shape), pipeline_mode=pl.Buffered(1))


def _mixer_in_kernel(x_ref, g_ref, w_in_ref, w_grp_ref, pscale_ref, w_pp_ref,
                     qT_ref, k_ref, vT_ref, ga_ref, gp_ref, ubuf_ref):
    i = pl.program_id(0)
    tm = x_ref.shape[0]
    h = _rms_norm(x_ref[...], g_ref[...], NORM_EPS).astype(_BF16)

    def proj(col, width):
        return jnp.dot(h, w_in_ref[:, col:col + width], preferred_element_type=_F32)

    q = proj(0, QK_WIDTH) * (HEAD_DIM ** -0.5)
    qT_ref[0] = q.T.astype(_BF16)
    k_ref[0] = proj(QK_WIDTH, QK_WIDTH).astype(_BF16)
    vT_ref[0] = proj(2 * QK_WIDTH, ATTN_WIDTH).T.astype(_BF16)

    @pl.when(i == 0)
    def _():
        ubuf_ref[0:MAX_WINDOW, :] = jnp.zeros((MAX_WINDOW, D_MODEL), _F32)

    ubuf_ref[MAX_WINDOW:MAX_WINDOW + tm, :] = proj(2 * QK_WIDTH + ATTN_WIDTH, D_MODEL)
    pos = i * tm + lax.broadcasted_iota(jnp.int32, (tm, 1), 0)
    y_pool = jnp.zeros((tm, D_MODEL), _F32)
    for g, w in enumerate(POOL_WINDOWS):
        cols = slice(g * POOL_GROUP_WIDTH, (g + 1) * POOL_GROUP_WIDTH)
        u = ubuf_ref[MAX_WINDOW:MAX_WINDOW + tm, cols]
        window_sum = u
        for s in range(1, w):
            window_sum = window_sum + ubuf_ref[MAX_WINDOW - s:MAX_WINDOW - s + tm, cols]
        count = jnp.minimum(pos + 1, w).astype(_F32)
        z = window_sum / count - u
        zg = jnp.dot(z.astype(_BF16), w_grp_ref[g], preferred_element_type=_F32)
        zg = zg * pscale_ref[:, cols]
        y_pool = y_pool + jnp.dot(zg.astype(_BF16), w_pp_ref[cols, :],
                                  preferred_element_type=_F32)
    ubuf_ref[0:MAX_WINDOW, :] = ubuf_ref[tm:tm + MAX_WINDOW, :]

    gate_col = 2 * QK_WIDTH + ATTN_WIDTH + D_MODEL
    ga_ref[...] = jax.nn.sigmoid(proj(gate_col, D_MODEL)).astype(_BF16)
    gp_ref[...] = jax.nn.sigmoid(proj(gate_col + D_MODEL, D_MODEL)) * y_pool


def _mixer_in(x2d, ln_g, w_in, w_grp, pool_scale, w_pp):
    seq = x2d.shape[0]
    tm = SEQ_TILE
    n_tiles = seq // tm
    in_width = w_in.shape[1]
    row_tile = lambda i: (i, 0)
    tile3 = lambda i: (i, 0, 0)
    return pl.pallas_call(
        _mixer_in_kernel,
        grid=(n_tiles,),
        in_specs=[
            pl.BlockSpec((tm, D_MODEL), row_tile),
            _resident((1, D_MODEL)),
            _resident((D_MODEL, in_width)),
            _resident(w_grp.shape),
            _resident((1, D_MODEL)),
            _resident((D_MODEL, D_MODEL)),
        ],
        out_specs=[
            pl.BlockSpec((1, QK_WIDTH, tm), tile3),
            pl.BlockSpec((1, tm, QK_WIDTH), tile3),
            pl.BlockSpec((1, ATTN_WIDTH, tm), tile3),
            pl.BlockSpec((tm, D_MODEL), row_tile),
            pl.BlockSpec((tm, D_MODEL), row_tile),
        ],
        out_shape=[
            jax.ShapeDtypeStruct((n_tiles, QK_WIDTH, tm), _BF16),
            jax.ShapeDtypeStruct((n_tiles, tm, QK_WIDTH), _BF16),
            jax.ShapeDtypeStruct((n_tiles, ATTN_WIDTH, tm), _BF16),
            jax.ShapeDtypeStruct((seq, D_MODEL), _BF16),
            jax.ShapeDtypeStruct((seq, D_MODEL), _F32),
        ],
        scratch_shapes=[pltpu.VMEM((tm + MAX_WINDOW, D_MODEL), _F32)],
        compiler_params=pltpu.CompilerParams(
            dimension_semantics=("arbitrary",), vmem_limit_bytes=VMEM_LIMIT_BYTES),
        name="mixer_in",
    )(x2d, ln_g, w_in, w_grp, pool_scale, w_pp)


def _lambda_kernel(lq1_ref, lk1_ref, lq2_ref, lk2_ref, lam_ref):
    a = jnp.sum(lq1_ref[...] * lk1_ref[...], axis=-1, keepdims=True)
    b = jnp.sum(lq2_ref[...] * lk2_ref[...], axis=-1, keepdims=True)
    lam_ref[...] = jnp.exp(a) - jnp.exp(b) + LAMBDA_INIT


def _diff_attn_kernel(lam_ref, qT_ref, k_ref, vT_ref, bias_ref, g_ref, o_ref,
                      m_ref, l_ref, acc_ref):
    i = pl.program_id(1)
    tq = qT_ref.shape[2]
    qT = qT_ref[0]
    row = lax.broadcasted_iota(jnp.int32, qT.shape, 0)
    zero = jnp.zeros_like(qT)
    q_maps = (jnp.where(row < HEAD_DIM, qT, zero), jnp.where(row >= HEAD_DIM, qT, zero))

    m_ref[...] = jnp.full(m_ref.shape, NEG_INF, _F32)
    l_ref[...] = jnp.zeros(l_ref.shape, _F32)
    acc_ref[...] = jnp.zeros(acc_ref.shape, _F32)

    def key_tile(j, bias):
        kt = k_ref[j]
        vt = vT_ref[j]
        for c in range(2):
            s = jnp.dot(kt, q_maps[c], preferred_element_type=_F32)
            if bias is not None:
                s = s + bias
            m_old = m_ref[c]
            m_new = jnp.maximum(m_old, jnp.max(s, axis=0, keepdims=True))
            alpha = jnp.exp(m_old - m_new)
            p = jnp.exp(s - m_new)
            l_ref[c] = alpha * l_ref[c] + jnp.sum(p, axis=0, keepdims=True)
            acc_ref[c] = alpha * acc_ref[c] + jnp.dot(vt, p.astype(_BF16),
                                                      preferred_element_type=_F32)
            m_ref[c] = m_new

    def far_tile(j, carry):
        key_tile(j, None)
        return carry

    lax.fori_loop(0, jnp.maximum(i - 1, 0), far_tile, 0)

    @pl.when(i >= 1)
    def _():
        key_tile(i - 1, bias_ref[0, 0])

    key_tile(i, bias_ref[0, 1])

    lam = lam_ref[0, 0]
    o = acc_ref[0] / l_ref[0] - lam * (acc_ref[1] / l_ref[1])
    ms = jnp.mean(o * o, axis=0, keepdims=True)
    o = o * lax.rsqrt(ms + SUBLN_EPS) * g_ref[...] * (1.0 - LAMBDA_INIT)
    o_ref[...] = o.T.astype(_BF16)


def _diff_attn(lam, qT, k, vT, bias, subln_g_col):
    n_tiles, _, t = qT.shape
    seq = n_tiles * t
    return pl.pallas_call(
        _diff_attn_kernel,
        grid=(N_HEADS, n_tiles),
        in_specs=[
            pl.BlockSpec(memory_space=pltpu.SMEM),
            pl.BlockSpec((1, 2 * HEAD_DIM, t), lambda h, i: (i, h, 0)),
            pl.BlockSpec((n_tiles, t, 2 * HEAD_DIM), lambda h, i: (0, 0, h)),
            pl.BlockSpec((n_tiles, V_HEAD_DIM, t), lambda h, i: (0, h, 0)),
            pl.BlockSpec((1, 2, t, t), lambda h, i: (h, 0, 0, 0)),
            pl.BlockSpec((V_HEAD_DIM, 1), lambda h, i: (0, 0)),
        ],
        out_specs=pl.BlockSpec((t, V_HEAD_DIM), lambda h, i: (i, h)),
        out_shape=jax.ShapeDtypeStruct((seq, ATTN_WIDTH), _BF16),
        scratch_shapes=[
            pltpu.VMEM((2, 1, t), _F32),
            pltpu.VMEM((2, 1, t), _F32),
            pltpu.VMEM((2, V_HEAD_DIM, t), _F32),
        ],
        compiler_params=pltpu.CompilerParams(
            dimension_semantics=("arbitrary", "arbitrary"), vmem_limit_bytes=VMEM_LIMIT_BYTES),
        name="diff_attn",
    )(lam, qT, k, vT, bias, subln_g_col)


def _t5_bucket(rel):
    nb = NUM_BUCKETS // 2
    max_exact = nb // 2
    bucket = (rel > 0).astype(jnp.int32) * nb
    n = jnp.abs(rel)
    n_f = jnp.maximum(n, 1).astype(jnp.float32)
    large = max_exact + (jnp.log(n_f / max_exact) / math.log(MAX_DISTANCE / max_exact)
                         * (nb - max_exact)).astype(jnp.int32)
    large = jnp.minimum(large, nb - 1)
    return bucket + jnp.where(n < max_exact, n, large)


def _near_bias_tiles(rel_table, t):
    assert t >= MAX_DISTANCE and t % CHUNK == 0
    key = jnp.arange(t, dtype=jnp.int32)[:, None]
    qry = jnp.arange(t, dtype=jnp.int32)[None, :]
    far = rel_table[_t5_bucket(jnp.int32(-(t + 1)))].astype(_F32)
    tiles = []
    for off in (-t, 0):
        b = rel_table[_t5_bucket(key + off - qry)].astype(_F32) - far
        visible = ((key + off) // CHUNK) <= (qry // CHUNK)
        tiles.append(jnp.where(visible[:, :, None], b, NEG_INF))
    return jnp.moveaxis(jnp.stack(tiles, axis=0), -1, 0)


def _mixer_out_kernel(x_ref, on_ref, ga_ref, gp_ref, w_pa_ref, w_out_ref, g_mlp_ref,
                      w_up_ref, w_down_ref, g_fin_ref, out_ref):
    y_attn = jnp.dot(on_ref[...], w_pa_ref[...], preferred_element_type=_F32)
    merged = ga_ref[...].astype(_F32) * y_attn + gp_ref[...]
    x1 = x_ref[...] + jnp.dot(merged.astype(_BF16), w_out_ref[...], preferred_element_type=_F32)
    h = _rms_norm(x1, g_mlp_ref[...], NORM_EPS).astype(_BF16)
    x2 = x1
    for c in range(0, D_FF, FF_CHUNK):
        up = jnp.dot(h, w_up_ref[:, c:c + FF_CHUNK], preferred_element_type=_F32)
        act = jnp.square(jnp.maximum(up, 0.0)).astype(_BF16)
        x2 = x2 + jnp.dot(act, w_down_ref[c:c + FF_CHUNK, :], preferred_element_type=_F32)
    out_ref[...] = _rms_norm(x2, g_fin_ref[...], NORM_EPS)


def _mixer_out(x2d, on, ga, gp, w_pa, w_out, g_mlp, w_up, w_down, g_fin):
    seq = x2d.shape[0]
    tm = SEQ_TILE
    row_tile = pl.BlockSpec((tm, D_MODEL), lambda i: (i, 0))
    return pl.pallas_call(
        _mixer_out_kernel,
        grid=(seq // tm,),
        in_specs=[
            row_tile, row_tile, row_tile, row_tile,
            _resident((ATTN_WIDTH, D_MODEL)),
            _resident((D_MODEL, D_MODEL)),
            _resident((1, D_MODEL)),
            _resident((D_MODEL, D_FF)),
            _resident((D_FF, D_MODEL)),
            _resident((1, D_MODEL)),
        ],
        out_specs=row_tile,
        out_shape=jax.ShapeDtypeStruct((seq, D_MODEL), _F32),
        compiler_params=pltpu.CompilerParams(
            dimension_semantics=("arbitrary",), vmem_limit_bytes=VMEM_LIMIT_BYTES),
        name="mixer_out",
    )(x2d, on, ga, gp, w_pa, w_out, g_mlp, w_up, w_down, g_fin)


def kernel(x, ln_mix_g, w_in, rel_bias_table, lambda_q1, lambda_k1, lambda_q2, lambda_k2,
           subln_g, w_proj_attn, w_pool_grp, pool_scale, w_proj_pool, w_out,
           ln_mlp_g, w_mlp_up, w_mlp_down, ln_final_g):
    batch, seq, d_model = x.shape
    assert batch == 1 and d_model == D_MODEL and seq % SEQ_TILE == 0
    assert ln_mix_g.shape[0] == 1, "single-layer trunk"
    x2d = x.reshape(seq, d_model)

    qT, k, vT, ga, gp = _mixer_in(
        x2d, ln_mix_g, w_in[0].astype(_BF16), w_pool_grp[0].astype(_BF16), pool_scale,
        w_proj_pool[0].astype(_BF16))

    lam = pl.pallas_call(
        _lambda_kernel, out_shape=jax.ShapeDtypeStruct((1, 1), _F32), name="diff_lambda",
    )(lambda_q1, lambda_k1, lambda_q2, lambda_k2)
    bias = _near_bias_tiles(rel_bias_table, SEQ_TILE)
    on = _diff_attn(lam, qT, k, vT, bias, subln_g.reshape(V_HEAD_DIM, 1))

    out = _mixer_out(
        x2d, on, ga, gp, w_proj_attn[0].astype(_BF16), w_out[0].astype(_BF16), ln_mlp_g,
        w_mlp_up[0].astype(_BF16), w_mlp_down[0].astype(_BF16), ln_final_g.reshape(1, d_model))
    return out.reshape(batch, seq, d_model)
```

```python
import functools
import math

import jax
import jax.numpy as jnp
from jax import lax
from jax.experimental import pallas as pl
from jax.experimental.pallas import tpu as pltpu

D_MODEL = 1024
N_HEADS = 8
HEAD_DIM = 64
V_HEAD_DIM = 2 * HEAD_DIM
QK_WIDTH = N_HEADS * 2 * HEAD_DIM
ATTN_WIDTH = N_HEADS * V_HEAD_DIM
POOL_WINDOWS = (2, 4, 8, 16)
POOL_GROUP_WIDTH = D_MODEL // len(POOL_WINDOWS)
MAX_WINDOW = max(POOL_WINDOWS)
D_FF = 4 * D_MODEL
CHUNK = 64
NUM_BUCKETS = 32
MAX_DISTANCE = 128
NORM_EPS = 1e-6
SUBLN_EPS = 1e-5
NEG_INF = -1e30
LAMBDA_INIT = 0.8 - 0.6 * math.exp(-0.3 * 0)
LOG2_E = math.log2(math.e)

SEQ_TILE = 512
FF_CHUNK = 1024
VMEM_LIMIT_BYTES = 56 * 1024 * 1024

_BF16 = jnp.bfloat16
_F32 = jnp.float32


def _rms_norm(x, g, eps):
    return x * lax.rsqrt(jnp.mean(x * x, axis=-1, keepdims=True) + eps) * g


def _resident(shape):
    return pl.BlockSpec(shape, lambda *_: (0,) * len(shape), pipeline_mode=pl.Buffered(1))


def _mixer_in_kernel(x_ref, g_ref, w_in_ref, w_grp_ref, pscale_ref, w_pp_ref,
                     qT_ref, k_ref, vT_ref, ga_ref, gp_ref, ubuf_ref):
    i = pl.program_id(0)
    tm = x_ref.shape[0]
    h = _rms_norm(x_ref[...], g_ref[...], NORM_EPS).astype(_BF16)

    def proj(col, width):
        return jnp.dot(h, w_in_ref[:, col:col + width], preferred_element_type=_F32)

    q = proj(0, QK_WIDTH) * (HEAD_DIM ** -0.5 * LOG2_E)
    qT_ref[0] = q.T.astype(_BF16)
    k_ref[0] = proj(QK_WIDTH, QK_WIDTH).astype(_BF16)
    vT_ref[0] = proj(2 * QK_WIDTH, ATTN_WIDTH).T.astype(_BF16)

    @pl.when(i == 0)
    def _():
        ubuf_ref[0:MAX_WINDOW, :] = jnp.zeros((MAX_WINDOW, D_MODEL), _F32)

    ubuf_ref[MAX_WINDOW:MAX_WINDOW + tm, :] = proj(2 * QK_WIDTH + ATTN_WIDTH, D_MODEL)
    pos = i * tm + lax.broadcasted_iota(jnp.int32, (tm, 1), 0)
    y_pool = jnp.zeros((tm, D_MODEL), _F32)
    for g, w in enumerate(POOL_WINDOWS):
        cols = slice(g * POOL_GROUP_WIDTH, (g + 1) * POOL_GROUP_WIDTH)
        u = ubuf_ref[MAX_WINDOW:MAX_WINDOW + tm, cols]
        window_sum = u
        for s in range(1, w):
            window_sum = window_sum + ubuf_ref[MAX_WINDOW - s:MAX_WINDOW - s + tm, cols]
        count = jnp.minimum(pos + 1, w).astype(_F32)
        z = window_sum / count - u
        zg = jnp.dot(z.astype(_BF16), w_grp_ref[g], preferred_element_type=_F32)
        zg = zg * pscale_ref[:, cols]
        y_pool = y_pool + jnp.dot(zg.astype(_BF16), w_pp_ref[cols, :],
                                  preferred_element_type=_F32)
    ubuf_ref[0:MAX_WINDOW, :] = ubuf_ref[tm:tm + MAX_WINDOW, :]

    gate_col = 2 * QK_WIDTH + ATTN_WIDTH + D_MODEL
    ga_ref[...] = jax.nn.sigmoid(proj(gate_col, D_MODEL)).astype(_BF16)
    gp_ref[...] = jax.nn.sigmoid(proj(gate_col + D_MODEL, D_MODEL)) * y_pool


def _mixer_in(x2d, ln_g, w_in, w_grp, pool_scale, w_pp):
    seq = x2d.shape[0]
    tm = SEQ_TILE
    n_tiles = seq // tm
    in_width = w_in.shape[1]
    row_tile = lambda i: (i, 0)
    tile3 = lambda i: (i, 0, 0)
    return pl.pallas_call(
        _mixer_in_kernel,
        grid=(n_tiles,),
        in_specs=[
            pl.BlockSpec((tm, D_MODEL), row_tile),
            _resident((1, D_MODEL)),
            _resident((D_MODEL, in_width)),
            _resident(w_grp.shape),
            _resident((1, D_MODEL)),
            _resident((D_MODEL, D_MODEL)),
        ],
        out_specs=[
            pl.BlockSpec((1, QK_WIDTH, tm), tile3),
            pl.BlockSpec((1, tm, QK_WIDTH), tile3),
            pl.BlockSpec((1, ATTN_WIDTH, tm), tile3),
            pl.BlockSpec((tm, D_MODEL), row_tile),
            pl.BlockSpec((tm, D_MODEL), row_tile),
        ],
        out_shape=[
            jax.ShapeDtypeStruct((n_tiles, QK_WIDTH, tm), _BF16),
            jax.ShapeDtypeStruct((n_tiles, tm, QK_WIDTH), _BF16),
            jax.ShapeDtypeStruct((n_tiles, ATTN_WIDTH, tm), _BF16),
            jax.ShapeDtypeStruct((seq, D_MODEL), _BF16),
            jax.ShapeDtypeStruct((seq, D_MODEL), _F32),
        ],
        scratch_shapes=[pltpu.VMEM((tm + MAX_WINDOW, D_MODEL), _F32)],
        compiler_params=pltpu.CompilerParams(
            dimension_semantics=("arbitrary",), vmem_limit_bytes=VMEM_LIMIT_BYTES),
        name="mixer_in",
    )(x2d, ln_g, w_in, w_grp, pool_scale, w_pp)


def _lambda_kernel(lq1_ref, lk1_ref, lq2_ref, lk2_ref, lam_ref):
    a = jnp.sum(lq1_ref[...] * lk1_ref[...], axis=-1, keepdims=True)
    b = jnp.sum(lq2_ref[...] * lk2_ref[...], axis=-1, keepdims=True)
    lam_ref[...] = jnp.exp(a) - jnp.exp(b) + LAMBDA_INIT


def _diff_attn_kernel(lam_ref, qT_ref, k_ref, vT_ref, bvec_ref, g_ref, o_ref,
                      qpad_ref, bias_ref, s0_ref, s1_ref, p0_ref, p1_ref, m_ref, l_ref, acc_ref):
    i = pl.program_id(1)
    t = qT_ref.shape[2]
    n_vec = bvec_ref.shape[2]

    @pl.when(i == 0)
    def _():
        rows = 64
        for kind, c0 in ((0, 2 * t - 1), (1, t - 1)):
            for r0 in range(0, t, rows):
                x = jnp.broadcast_to(bvec_ref[0], (rows, n_vec))
                x = pltpu.roll(x, (n_vec - c0 + r0) % n_vec, 1, stride=1, stride_axis=0)[:, :t]
                if kind == 1:
                    key = r0 + lax.broadcasted_iota(jnp.int32, (rows, t), 0)
                    qry = lax.broadcasted_iota(jnp.int32, (rows, t), 1)
                    x = jnp.where(key // CHUNK <= qry // CHUNK, x, NEG_INF)
                bias_ref[kind, r0:r0 + rows, :] = x

    qT = qT_ref[0]
    row = lax.broadcasted_iota(jnp.int32, qT.shape, 0)
    zero = jnp.zeros_like(qT)
    qpad_ref[0] = jnp.where(row < HEAD_DIM, qT, zero)
    qpad_ref[1] = jnp.where(row >= HEAD_DIM, qT, zero)

    m_ref[...] = jnp.full(m_ref.shape, NEG_INF, _F32)
    l_ref[...] = jnp.zeros(l_ref.shape, _F32)
    acc_ref[...] = jnp.zeros(acc_ref.shape, _F32)
    p1_ref[...] = jnp.zeros(p1_ref.shape, _BF16)
    no_rescale = (jnp.ones((1, t), _F32),) * 2

    slots = ((s0_ref, p0_ref), (s1_ref, p1_ref))

    def scores(tile, slot):
        kt = k_ref[tile]
        for c in range(2):
            slots[slot][0][c] = jnp.dot(kt, qpad_ref[c], preferred_element_type=_F32)

    def softmax(slot, bias_kind):
        s_ref, p_ref = slots[slot]
        alphas = []
        for c in range(2):
            s = s_ref[c]
            if bias_kind is not None:
                s = s + bias_ref[bias_kind]
            m_old = m_ref[c]
            m_new = jnp.maximum(m_old, jnp.max(s, axis=0, keepdims=True))
            alpha = jnp.exp2(m_old - m_new)
            p = jnp.exp2(s - m_new)
            l_ref[c] = alpha * l_ref[c] + jnp.sum(p, axis=0, keepdims=True)
            p_ref[c] = p.astype(_BF16)
            m_ref[c] = m_new
            alphas.append(alpha)
        return tuple(alphas)

    def accumulate(tile, slot, alphas):
        p_ref = slots[slot][1]
        vt = vT_ref[jnp.maximum(tile, 0)]
        for c in range(2):
            acc_ref[c] = alphas[c] * acc_ref[c] + jnp.dot(vt, p_ref[c],
                                                          preferred_element_type=_F32)

    n_tiles = i + 1
    n_far = jnp.maximum(i - 1, 0)
    scores(0, 0)

    def far_pair(jj, pending):
        t0 = 2 * jj
        scores(t0 + 1, 1)
        alphas = softmax(0, None)
        accumulate(t0 - 1, 1, pending)
        scores(t0 + 2, 0)
        pending = softmax(1, None)
        accumulate(t0, 0, alphas)
        return pending

    pending = lax.fori_loop(0, n_far // 2, far_pair, no_rescale)
    base = 2 * (n_far // 2)

    def tail(kinds):
        alphas = pending
        for idx, kind in enumerate(kinds):
            cur, oth = idx % 2, (idx + 1) % 2
            accumulate(base + idx - 1, oth, alphas)
            if idx + 1 < len(kinds):
                scores(base + idx + 1, oth)
            alphas = softmax(cur, kind)
        accumulate(base + len(kinds) - 1, (len(kinds) - 1) % 2, alphas)

    @pl.when(n_tiles - base == 1)
    def _():
        tail((1,))

    @pl.when(n_tiles - base == 2)
    def _():
        tail((0, 1))

    @pl.when(n_tiles - base == 3)
    def _():
        tail((None, 0, 1))

    lam = lam_ref[0, 0]
    o = acc_ref[0] / l_ref[0] - lam * (acc_ref[1] / l_ref[1])
    ms = jnp.mean(o * o, axis=0, keepdims=True)
    o = o * lax.rsqrt(ms + SUBLN_EPS) * g_ref[...] * (1.0 - LAMBDA_INIT)
    o_ref[...] = o.T.astype(_BF16)


def _diff_attn(lam, qT, k, vT, bias_vec, subln_g_col):
    n_tiles, _, t = qT.shape
    seq = n_tiles * t
    n_vec = bias_vec.shape[2]
    return pl.pallas_call(
        _diff_attn_kernel,
        grid=(N_HEADS, n_tiles),
        in_specs=[
            pl.BlockSpec(memory_space=pltpu.SMEM),
            pl.BlockSpec((1, 2 * HEAD_DIM, t), lambda h, i: (i, h, 0)),
            pl.BlockSpec((n_tiles, t, 2 * HEAD_DIM), lambda h, i: (0, 0, h)),
            pl.BlockSpec((n_tiles, V_HEAD_DIM, t), lambda h, i: (0, h, 0)),
            pl.BlockSpec((1, 1, n_vec), lambda h, i: (h, 0, 0)),
            pl.BlockSpec((V_HEAD_DIM, 1), lambda h, i: (0, 0)),
        ],
        out_specs=pl.BlockSpec((t, V_HEAD_DIM), lambda h, i: (i, h)),
        out_shape=jax.ShapeDtypeStruct((seq, ATTN_WIDTH), _BF16),
        scratch_shapes=[
            pltpu.VMEM((2, 2 * HEAD_DIM, t), _BF16),
            pltpu.VMEM((2, t, t), _F32),
            pltpu.VMEM((2, t, t), _F32),
            pltpu.VMEM((2, t, t), _F32),
            pltpu.VMEM((2, t, t), _BF16),
            pltpu.VMEM((2, t, t), _BF16),
            pltpu.VMEM((2, 1, t), _F32),
            pltpu.VMEM((2, 1, t), _F32),
            pltpu.VMEM((2, V_HEAD_DIM, t), _F32),
        ],
        compiler_params=pltpu.CompilerParams(
            dimension_semantics=("arbitrary", "arbitrary"), vmem_limit_bytes=VMEM_LIMIT_BYTES),
        name="diff_attn",
    )(lam, qT, k, vT, bias_vec, subln_g_col)


def _t5_bucket(rel):
    nb = NUM_BUCKETS // 2
    max_exact = nb // 2
    bucket = (rel > 0).astype(jnp.int32) * nb
    n = jnp.abs(rel)
    n_f = jnp.maximum(n, 1).astype(jnp.float32)
    large = max_exact + (jnp.log(n_f / max_exact) / math.log(MAX_DISTANCE / max_exact)
                         * (nb - max_exact)).astype(jnp.int32)
    large = jnp.minimum(large, nb - 1)
    return bucket + jnp.where(n < max_exact, n, large)


def _near_bias_vector(rel_table, t):
    assert t >= MAX_DISTANCE and t % CHUNK == 0
    rel = jnp.arange(3 * t - 1, dtype=jnp.int32) - (2 * t - 1)
    far = rel_table[_t5_bucket(jnp.int32(-(t + 1)))].astype(_F32)
    vec = (rel_table[_t5_bucket(rel)].astype(_F32) - far) * LOG2_E
    vec = jnp.pad(vec[::-1], ((0, 1), (0, 0)))
    return vec.T.reshape(rel_table.shape[1], 1, 3 * t)


def _mixer_out_kernel(x_ref, on_ref, ga_ref, gp_ref, w_pa_ref, w_out_ref, g_mlp_ref,
                      w_up_ref, w_down_ref, g_fin_ref, out_ref):
    y_attn = jnp.dot(on_ref[...], w_pa_ref[...], preferred_element_type=_F32)
    merged = ga_ref[...].astype(_F32) * y_attn + gp_ref[...]
    x1 = x_ref[...] + jnp.dot(merged.astype(_BF16), w_out_ref[...], preferred_element_type=_F32)
    h = _rms_norm(x1, g_mlp_ref[...], NORM_EPS).astype(_BF16)
    x2 = x1
    for c in range(0, D_FF, FF_CHUNK):
        up = jnp.dot(h, w_up_ref[:, c:c + FF_CHUNK], preferred_element_type=_F32)
        act = jnp.square(jnp.maximum(up, 0.0)).astype(_BF16)
        x2 = x2 + jnp.dot(act, w_down_ref[c:c + FF_CHUNK, :], preferred_element_type=_F32)
    out_ref[...] = _rms_norm(x2, g_fin_ref[...], NORM_EPS)


def _mixer_out(x2d, on, ga, gp, w_pa, w_out, g_mlp, w_up, w_down, g_fin):
    seq = x2d.shape[0]
    tm = SEQ_TILE
    row_tile = pl.BlockSpec((tm, D_MODEL), lambda i: (i, 0))
    return pl.pallas_call(
        _mixer_out_kernel,
        grid=(seq // tm,),
        in_specs=[
            row_tile, row_tile, row_tile, row_tile,
            _resident((ATTN_WIDTH, D_MODEL)),
            _resident((D_MODEL, D_MODEL)),
            _resident((1, D_MODEL)),
            _resident((D_MODEL, D_FF)),
            _resident((D_FF, D_MODEL)),
            _resident((1, D_MODEL)),
        ],
        out_specs=row_tile,
        out_shape=jax.ShapeDtypeStruct((seq, D_MODEL), _F32),
        compiler_params=pltpu.CompilerParams(
            dimension_semantics=("arbitrary",), vmem_limit_bytes=VMEM_LIMIT_BYTES),
        name="mixer_out",
    )(x2d, on, ga, gp, w_pa, w_out, g_mlp, w_up, w_down, g_fin)


def kernel(x, ln_mix_g, w_in, rel_bias_table, lambda_q1, lambda_k1, lambda_q2, lambda_k2,
           subln_g, w_proj_attn, w_pool_grp, pool_scale, w_proj_pool, w_out,
           ln_mlp_g, w_mlp_up, w_mlp_down, ln_final_g):
    batch, seq, d_model = x.shape
    assert batch == 1 and d_model == D_MODEL and seq % SEQ_TILE == 0
    assert ln_mix_g.shape[0] == 1, "single-layer trunk"
    x2d = x.reshape(seq, d_model)

    qT, k, vT, ga, gp = _mixer_in(
        x2d, ln_mix_g, w_in[0].astype(_BF16), w_pool_grp[0].astype(_BF16), pool_scale,
        w_proj_pool[0].astype(_BF16))

    lam = pl.pallas_call(
        _lambda_kernel, out_shape=jax.ShapeDtypeStruct((1, 1), _F32), name="diff_lambda",
    )(lambda_q1, lambda_k1, lambda_q2, lambda_k2)
    bias_vec = _near_bias_vector(rel_bias_table, SEQ_TILE)
    on = _diff_attn(lam, qT, k, vT, bias_vec, subln_g.reshape(V_HEAD_DIM, 1))

    out = _mixer_out(
        x2d, on, ga, gp, w_proj_attn[0].astype(_BF16), w_out[0].astype(_BF16), ln_mlp_g,
        w_mlp_up[0].astype(_BF16), w_mlp_down[0].astype(_BF16), ln_final_g.reshape(1, d_model))
    return out.reshape(batch, seq, d_model)
```

```python
import functools
import math

import jax
import jax.numpy as jnp
from jax import lax
from jax.experimental import pallas as pl
from jax.experimental.pallas import tpu as pltpu

D_MODEL = 1024
N_HEADS = 8
HEAD_DIM = 64
V_HEAD_DIM = 2 * HEAD_DIM
QK_WIDTH = N_HEADS * 2 * HEAD_DIM
ATTN_WIDTH = N_HEADS * V_HEAD_DIM
POOL_WINDOWS = (2, 4, 8, 16)
POOL_GROUP_WIDTH = D_MODEL // len(POOL_WINDOWS)
MAX_WINDOW = max(POOL_WINDOWS)
D_FF = 4 * D_MODEL
CHUNK = 64
NUM_BUCKETS = 32
MAX_DISTANCE = 128
NORM_EPS = 1e-6
SUBLN_EPS = 1e-5
NEG_INF = -1e30
LAMBDA_INIT = 0.8 - 0.6 * math.exp(-0.3 * 0)
LOG2_E = math.log2(math.e)

SEQ_TILE = 512
FF_CHUNK = 1024
VMEM_LIMIT_BYTES = 56 * 1024 * 1024

_BF16 = jnp.bfloat16
_F32 = jnp.float32


def _rms_norm(x, g, eps):
    return x * lax.rsqrt(jnp.mean(x * x, axis=-1, keepdims=True) + eps) * g


def _resident(shape):
    return pl.BlockSpec(shape, lambda *_: (0,) * len(shape), pipeline_mode=pl.Buffered(1))


def _mixer_in_kernel(x_ref, g_ref, w_in_ref, w_grp_ref, pscale_ref, w_pp_ref,
                     qT_ref, k_ref, vT_ref, ga_ref, gp_ref, ubuf_ref):
    i = pl.program_id(0)
    tm = x_ref.shape[0]
    h = _rms_norm(x_ref[...], g_ref[...], NORM_EPS).astype(_BF16)

    def proj(col, width):
        return jnp.dot(h, w_in_ref[:, col:col + width], preferred_element_type=_F32)

    q = proj(0, QK_WIDTH) * (HEAD_DIM ** -0.5 * LOG2_E)
    qT_ref[0] = q.T.astype(_BF16)
    k_ref[0] = proj(QK_WIDTH, QK_WIDTH).astype(_BF16)
    vT_ref[0] = proj(2 * QK_WIDTH, ATTN_WIDTH).T.astype(_BF16)

    @pl.when(i == 0)
    def _():
        ubuf_ref[0:MAX_WINDOW, :] = jnp.zeros((MAX_WINDOW, D_MODEL), _F32)

    ubuf_ref[MAX_WINDOW:MAX_WINDOW + tm, :] = proj(2 * QK_WIDTH + ATTN_WIDTH, D_MODEL)
    pos = i * tm + lax.broadcasted_iota(jnp.int32, (tm, 1), 0)
    y_pool = jnp.zeros((tm, D_MODEL), _F32)
    for g, w in enumerate(POOL_WINDOWS):
        cols = slice(g * POOL_GROUP_WIDTH, (g + 1) * POOL_GROUP_WIDTH)
        u = ubuf_ref[MAX_WINDOW:MAX_WINDOW + tm, cols]
        window_sum = u
        for s in range(1, w):
            window_sum = window_sum + ubuf_ref[MAX_WINDOW - s:MAX_WINDOW - s + tm, cols]
        count = jnp.minimum(pos + 1, w).astype(_F32)
        z = window_sum / count - u
        zg = jnp.dot(z.astype(_BF16), w_grp_ref[g], preferred_element_type=_F32)
        zg = zg * pscale_ref[:, cols]
        y_pool = y_pool + jnp.dot(zg.astype(_BF16), w_pp_ref[cols, :],
                                  preferred_element_type=_F32)
    ubuf_ref[0:MAX_WINDOW, :] = ubuf_ref[tm:tm + MAX_WINDOW, :]

    gate_col = 2 * QK_WIDTH + ATTN_WIDTH + D_MODEL
    ga_ref[...] = jax.nn.sigmoid(proj(gate_col, D_MODEL)).astype(_BF16)
    gp_ref[...] = jax.nn.sigmoid(proj(gate_col + D_MODEL, D_MODEL)) * y_pool


def _mixer_in(x2d, ln_g, w_in, w_grp, pool_scale, w_pp):
    seq = x2d.shape[0]
    tm = SEQ_TILE
    n_tiles = seq // tm
    in_width = w_in.shape[1]
    row_tile = lambda i: (i, 0)
    tile3 = lambda i: (i, 0, 0)
    return pl.pallas_call(
        _mixer_in_kernel,
        grid=(n_tiles,),
        in_specs=[
            pl.BlockSpec((tm, D_MODEL), row_tile),
            _resident((1, D_MODEL)),
            _resident((D_MODEL, in_width)),
            _resident(w_grp.shape),
            _resident((1, D_MODEL)),
            _resident((D_MODEL, D_MODEL)),
        ],
        out_specs=[
            pl.BlockSpec((1, QK_WIDTH, tm), tile3),
            pl.BlockSpec((1, tm, QK_WIDTH), tile3),
            pl.BlockSpec((1, ATTN_WIDTH, tm), tile3),
            pl.BlockSpec((tm, D_MODEL), row_tile),
            pl.BlockSpec((tm, D_MODEL), row_tile),
        ],
        out_shape=[
            jax.ShapeDtypeStruct((n_tiles, QK_WIDTH, tm), _BF16),
            jax.ShapeDtypeStruct((n_tiles, tm, QK_WIDTH), _BF16),
            jax.ShapeDtypeStruct((n_tiles, ATTN_WIDTH, tm), _BF16),
            jax.ShapeDtypeStruct((seq, D_MODEL), _BF16),
            jax.ShapeDtypeStruct((seq, D_MODEL), _F32),
        ],
        scratch_shapes=[pltpu.VMEM((tm + MAX_WINDOW, D_MODEL), _F32)],
        compiler_params=pltpu.CompilerParams(
            dimension_semantics=("arbitrary",), vmem_limit_bytes=VMEM_LIMIT_BYTES),
        name="mixer_in",
    )(x2d, ln_g, w_in, w_grp, pool_scale, w_pp)


def _lambda_kernel(lq1_ref, lk1_ref, lq2_ref, lk2_ref, lam_ref):
    a = jnp.sum(lq1_ref[...] * lk1_ref[...], axis=-1, keepdims=True)
    b = jnp.sum(lq2_ref[...] * lk2_ref[...], axis=-1, keepdims=True)
    lam_ref[...] = jnp.exp(a) - jnp.exp(b) + LAMBDA_INIT


def _diff_attn_kernel(lam_ref, qT_ref, k_ref, vT_ref, bvec_ref, g_ref, o_ref,
                      qpad_ref, bias_ref, s_ref, p_ref, m_ref, l_ref, acc_ref):
    i = pl.program_id(1)
    t = qT_ref.shape[2]
    n_vec = bvec_ref.shape[2]

    @pl.when(i == 0)
    def _():
        rows = 64
        for kind, c0 in ((0, 2 * t - 1), (1, t - 1)):
            for r0 in range(0, t, rows):
                x = jnp.broadcast_to(bvec_ref[0], (rows, n_vec))
                x = pltpu.roll(x, (n_vec - c0 + r0) % n_vec, 1, stride=1, stride_axis=0)[:, :t]
                if kind == 1:
                    key = r0 + lax.broadcasted_iota(jnp.int32, (rows, t), 0)
                    qry = lax.broadcasted_iota(jnp.int32, (rows, t), 1)
                    x = jnp.where(key // CHUNK <= qry // CHUNK, x, NEG_INF)
                bias_ref[kind, r0:r0 + rows, :] = x

    qT = qT_ref[0]
    row = lax.broadcasted_iota(jnp.int32, qT.shape, 0)
    zero = jnp.zeros_like(qT)
    qpad_ref[0] = jnp.where(row < HEAD_DIM, qT, zero)
    qpad_ref[1] = jnp.where(row >= HEAD_DIM, qT, zero)

    m_ref[...] = jnp.full(m_ref.shape, NEG_INF, _F32)
    l_ref[...] = jnp.zeros(l_ref.shape, _F32)
    acc_ref[...] = jnp.zeros(acc_ref.shape, _F32)
    p_ref[...] = jnp.zeros(p_ref.shape, _BF16)
    no_rescale = (jnp.ones((1, t), _F32),) * 2

    def scores(tile):
        kt = k_ref[tile]
        tile_max = []
        for c in range(2):
            s = jnp.dot(kt, qpad_ref[c], preferred_element_type=_F32)
            s_ref[c] = s
            tile_max.append(jnp.max(s, axis=0, keepdims=True))
        return tuple(tile_max)

    def softmax(bias_kind, tile_max):
        alphas = []
        for c in range(2):
            s = s_ref[c]
            if bias_kind is None:
                s_max = tile_max[c]
            else:
                s = s + bias_ref[bias_kind]
                s_max = jnp.max(s, axis=0, keepdims=True)
            m_old = m_ref[c]
            m_new = jnp.maximum(m_old, s_max)
            alpha = jnp.exp2(m_old - m_new)
            p = jnp.exp2(s - m_new)
            l_ref[c] = alpha * l_ref[c] + jnp.sum(p, axis=0, keepdims=True)
            p_ref[c] = p.astype(_BF16)
            m_ref[c] = m_new
            alphas.append(alpha)
        return tuple(alphas)

    def accumulate(tile, alphas):
        vt = vT_ref[jnp.maximum(tile, 0)]
        for c in range(2):
            acc_ref[c] = alphas[c] * acc_ref[c] + jnp.dot(vt, p_ref[c],
                                                          preferred_element_type=_F32)

    n_far = jnp.maximum(i - 1, 0)
    first_max = scores(0)

    def far_step(tile, carry):
        pending, tile_max = carry
        accumulate(tile - 1, pending)
        pending = softmax(None, tile_max)
        return pending, scores(tile + 1)

    pending, _ = lax.fori_loop(0, n_far, far_step, (no_rescale, first_max))

    @pl.when(i == 0)
    def _():
        accumulate(0, softmax(1, None))

    @pl.when(i > 0)
    def _():
        accumulate(i - 2, pending)
        alphas = softmax(0, None)
        scores(i)
        accumulate(i - 1, alphas)
        accumulate(i, softmax(1, None))

    lam = lam_ref[0, 0]
    o = acc_ref[0] / l_ref[0] - lam * (acc_ref[1] / l_ref[1])
    ms = jnp.mean(o * o, axis=0, keepdims=True)
    o = o * lax.rsqrt(ms + SUBLN_EPS) * g_ref[...] * (1.0 - LAMBDA_INIT)
    o_ref[...] = o.T.astype(_BF16)


def _diff_attn(lam, qT, k, vT, bias_vec, subln_g_col):
    n_tiles, _, t = qT.shape
    seq = n_tiles * t
    n_vec = bias_vec.shape[2]
    return pl.pallas_call(
        _diff_attn_kernel,
        grid=(N_HEADS, n_tiles),
        in_specs=[
            pl.BlockSpec(memory_space=pltpu.SMEM),
            pl.BlockSpec((1, 2 * HEAD_DIM, t), lambda h, i: (i, h, 0)),
            pl.BlockSpec((n_tiles, t, 2 * HEAD_DIM), lambda h, i: (0, 0, h)),
            pl.BlockSpec((n_tiles, V_HEAD_DIM, t), lambda h, i: (0, h, 0)),
            pl.BlockSpec((1, 1, n_vec), lambda h, i: (h, 0, 0)),
            pl.BlockSpec((V_HEAD_DIM, 1), lambda h, i: (0, 0)),
        ],
        out_specs=pl.BlockSpec((t, V_HEAD_DIM), lambda h, i: (i, h)),
        out_shape=jax.ShapeDtypeStruct((seq, ATTN_WIDTH), _BF16),
        scratch_shapes=[
            pltpu.VMEM((2, 2 * HEAD_DIM, t), _BF16),
            pltpu.VMEM((2, t, t), _F32),
            pltpu.VMEM((2, t, t), _F32),
            pltpu.VMEM((2, t, t), _BF16),
            pltpu.VMEM((2, 1, t), _F32),
            pltpu.VMEM((2, 1, t), _F32),
            pltpu.VMEM((2, V_HEAD_DIM, t), _F32),
        ],
        compiler_params=pltpu.CompilerParams(
            dimension_semantics=("arbitrary", "arbitrary"), vmem_limit_bytes=VMEM_LIMIT_BYTES),
        name="diff_attn",
    )(lam, qT, k, vT, bias_vec, subln_g_col)


def _t5_bucket(rel):
    nb = NUM_BUCKETS // 2
    max_exact = nb // 2
    bucket = (rel > 0).astype(jnp.int32) * nb
    n = jnp.abs(rel)
    n_f = jnp.maximum(n, 1).astype(jnp.float32)
    large = max_exact + (jnp.log(n_f / max_exact) / math.log(MAX_DISTANCE / max_exact)
                         * (nb - max_exact)).astype(jnp.int32)
    large = jnp.minimum(large, nb - 1)
    return bucket + jnp.where(n < max_exact, n, large)


def _near_bias_vector(rel_table, t):
    assert t >= MAX_DISTANCE and t % CHUNK == 0
    rel = jnp.arange(3 * t - 1, dtype=jnp.int32) - (2 * t - 1)
    far = rel_table[_t5_bucket(jnp.int32(-(t + 1)))].astype(_F32)
    vec = (rel_table[_t5_bucket(rel)].astype(_F32) - far) * LOG2_E
    vec = jnp.pad(vec[::-1], ((0, 1), (0, 0)))
    return vec.T.reshape(rel_table.shape[1], 1, 3 * t)


def _mixer_out_kernel(x_ref, on_ref, ga_ref, gp_ref, w_pa_ref, w_out_ref, g_mlp_ref,
                      w_up_ref, w_down_ref, g_fin_ref, out_ref):
    y_attn = jnp.dot(on_ref[...], w_pa_ref[...], preferred_element_type=_F32)
    merged = ga_ref[...].astype(_F32) * y_attn + gp_ref[...]
    x1 = x_ref[...] + jnp.dot(merged.astype(_BF16), w_out_ref[...], preferred_element_type=_F32)
    h = _rms_norm(x1, g_mlp_ref[...], NORM_EPS).astype(_BF16)
    x2 = x1
    for c in range(0, D_FF, FF_CHUNK):
        up = jnp.dot(h, w_up_ref[:, c:c + FF_CHUNK], preferred_element_type=_F32)
        act = jnp.square(jnp.maximum(up, 0.0)).astype(_BF16)
        x2 = x2 + jnp.dot(act, w_down_ref[c:c + FF_CHUNK, :], preferred_element_type=_F32)
    out_ref[...] = _rms_norm(x2, g_fin_ref[...], NORM_EPS)


def _mixer_out(x2d, on, ga, gp, w_pa, w_out, g_mlp, w_up, w_down, g_fin):
    seq = x2d.shape[0]
    tm = SEQ_TILE
    row_tile = pl.BlockSpec((tm, D_MODEL), lambda i: (i, 0))
    return pl.pallas_call(
        _mixer_out_kernel,
        grid=(seq // tm,),
        in_specs=[
            row_tile, row_tile, row_tile, row_tile,
            _resident((ATTN_WIDTH, D_MODEL)),
            _resident((D_MODEL, D_MODEL)),
            _resident((1, D_MODEL)),
            _resident((D_MODEL, D_FF)),
            _resident((D_FF, D_MODEL)),
            _resident((1, D_MODEL)),
        ],
        out_specs=row_tile,
        out_shape=jax.ShapeDtypeStruct((seq, D_MODEL), _F32),
        compiler_params=pltpu.CompilerParams(
            dimension_semantics=("arbitrary",), vmem_limit_bytes=VMEM_LIMIT_BYTES),
        name="mixer_out",
    )(x2d, on, ga, gp, w_pa, w_out, g_mlp, w_up, w_down, g_fin)


def kernel(x, ln_mix_g, w_in, rel_bias_table, lambda_q1, lambda_k1, lambda_q2, lambda_k2,
           subln_g, w_proj_attn, w_pool_grp, pool_scale, w_proj_pool, w_out,
           ln_mlp_g, w_mlp_up, w_mlp_down, ln_final_g):
    batch, seq, d_model = x.shape
    assert batch == 1 and d_model == D_MODEL and seq % SEQ_TILE == 0
    assert ln_mix_g.shape[0] == 1, "single-layer trunk"
    x2d = x.reshape(seq, d_model)

    qT, k, vT, ga, gp = _mixer_in(
        x2d, ln_mix_g, w_in[0].astype(_BF16), w_pool_grp[0].astype(_BF16), pool_scale,
        w_proj_pool[0].astype(_BF16))

    lam = pl.pallas_call(
        _lambda_kernel, out_shape=jax.ShapeDtypeStruct((1, 1), _F32), name="diff_lambda",
    )(lambda_q1, lambda_k1, lambda_q2, lambda_k2)
    bias_vec = _near_bias_vector(rel_bias_table, SEQ_TILE)
    on = _diff_attn(lam, qT, k, vT, bias_vec, subln_g.reshape(V_HEAD_DIM, 1))

    out = _mixer_out(
        x2d, on, ga, gp, w_proj_attn[0].astype(_BF16), w_out[0].astype(_BF16), ln_mlp_g,
        w_mlp_up[0].astype(_BF16), w_mlp_down[0].astype(_BF16), ln_final_g.reshape(1, d_model))
    return out.reshape(batch, seq, d_model)
```

```python
import functools
import math

import jax
import jax.numpy as jnp
from jax import lax
from jax.experimental import pallas as pl
from jax.experimental.pallas import tpu as pltpu

D_MODEL = 1024
N_HEADS = 8
HEAD_DIM = 64
V_HEAD_DIM = 2 * HEAD_DIM
QK_WIDTH = N_HEADS * 2 * HEAD_DIM
ATTN_WIDTH = N_HEADS * V_HEAD_DIM
BF16_SUBLANES = 16
V_AUG_DIM = V_HEAD_DIM + BF16_SUBLANES
POOL_WINDOWS = (2, 4, 8, 16)
POOL_GROUP_WIDTH = D_MODEL // len(POOL_WINDOWS)
MAX_WINDOW = max(POOL_WINDOWS)
D_FF = 4 * D_MODEL
CHUNK = 64
NUM_BUCKETS = 32
MAX_DISTANCE = 128
NORM_EPS = 1e-6
SUBLN_EPS = 1e-5
NEG_INF = -1e30
LAMBDA_INIT = 0.8 - 0.6 * math.exp(-0.3 * 0)
LOG2_E = math.log2(math.e)

SEQ_TILE = 512
FF_CHUNK = 1024
VMEM_LIMIT_BYTES = 56 * 1024 * 1024

_BF16 = jnp.bfloat16
_F32 = jnp.float32


def _rms_norm(x, g, eps):
    return x * lax.rsqrt(jnp.mean(x * x, axis=-1, keepdims=True) + eps) * g


def _resident(shape):
    return pl.BlockSpec(shape, lambda *_: (0,) * len(shape), pipeline_mode=pl.Buffered(1))


def _mixer_in_kernel(x_ref, g_ref, w_in_ref, w_grp_ref, pscale_ref, w_pp_ref,
                     qT_ref, k_ref, vT_ref, ga_ref, gp_ref, ubuf_ref):
    i = pl.program_id(0)
    tm = x_ref.shape[0]
    h = _rms_norm(x_ref[...], g_ref[...], NORM_EPS).astype(_BF16)

    def proj(col, width):
        return jnp.dot(h, w_in_ref[:, col:col + width], preferred_element_type=_F32)

    q = proj(0, QK_WIDTH) * (HEAD_DIM ** -0.5 * LOG2_E)
    qT_ref[0] = q.T.astype(_BF16)
    k_ref[0] = proj(QK_WIDTH, QK_WIDTH).astype(_BF16)
    vT = proj(2 * QK_WIDTH, ATTN_WIDTH).T.astype(_BF16)
    ones = jnp.ones((BF16_SUBLANES, tm), _BF16)
    for hd in range(N_HEADS):
        vT_ref[0, hd * V_AUG_DIM:hd * V_AUG_DIM + V_HEAD_DIM, :] = (
            vT[hd * V_HEAD_DIM:(hd + 1) * V_HEAD_DIM, :])
        vT_ref[0, hd * V_AUG_DIM + V_HEAD_DIM:(hd + 1) * V_AUG_DIM, :] = ones

    @pl.when(i == 0)
    def _():
        ubuf_ref[0:MAX_WINDOW, :] = jnp.zeros((MAX_WINDOW, D_MODEL), _F32)

    ubuf_ref[MAX_WINDOW:MAX_WINDOW + tm, :] = proj(2 * QK_WIDTH + ATTN_WIDTH, D_MODEL)
    pos = i * tm + lax.broadcasted_iota(jnp.int32, (tm, 1), 0)
    y_pool = jnp.zeros((tm, D_MODEL), _F32)
    for g, w in enumerate(POOL_WINDOWS):
        cols = slice(g * POOL_GROUP_WIDTH, (g + 1) * POOL_GROUP_WIDTH)
        u = ubuf_ref[MAX_WINDOW:MAX_WINDOW + tm, cols]
        window_sum = u
        for s in range(1, w):
            window_sum = window_sum + ubuf_ref[MAX_WINDOW - s:MAX_WINDOW - s + tm, cols]
        count = jnp.minimum(pos + 1, w).astype(_F32)
        z = window_sum / count - u
        zg = jnp.dot(z.astype(_BF16), w_grp_ref[g], preferred_element_type=_F32)
        zg = zg * pscale_ref[:, cols]
        y_pool = y_pool + jnp.dot(zg.astype(_BF16), w_pp_ref[cols, :],
                                  preferred_element_type=_F32)
    ubuf_ref[0:MAX_WINDOW, :] = ubuf_ref[tm:tm + MAX_WINDOW, :]

    gate_col = 2 * QK_WIDTH + ATTN_WIDTH + D_MODEL
    ga_ref[...] = jax.nn.sigmoid(proj(gate_col, D_MODEL)).astype(_BF16)
    gp_ref[...] = jax.nn.sigmoid(proj(gate_col + D_MODEL, D_MODEL)) * y_pool


def _mixer_in(x2d, ln_g, w_in, w_grp, pool_scale, w_pp):
    seq = x2d.shape[0]
    tm = SEQ_TILE
    n_tiles = seq // tm
    in_width = w_in.shape[1]
    row_tile = lambda i: (i, 0)
    tile3 = lambda i: (i, 0, 0)
    return pl.pallas_call(
        _mixer_in_kernel,
        grid=(n_tiles,),
        in_specs=[
            pl.BlockSpec((tm, D_MODEL), row_tile),
            _resident((1, D_MODEL)),
            _resident((D_MODEL, in_width)),
            _resident(w_grp.shape),
            _resident((1, D_MODEL)),
            _resident((D_MODEL, D_MODEL)),
        ],
        out_specs=[
            pl.BlockSpec((1, QK_WIDTH, tm), tile3),
            pl.BlockSpec((1, tm, QK_WIDTH), tile3),
            pl.BlockSpec((1, N_HEADS * V_AUG_DIM, tm), tile3),
            pl.BlockSpec((tm, D_MODEL), row_tile),
            pl.BlockSpec((tm, D_MODEL), row_tile),
        ],
        out_shape=[
            jax.ShapeDtypeStruct((n_tiles, QK_WIDTH, tm), _BF16),
            jax.ShapeDtypeStruct((n_tiles, tm, QK_WIDTH), _BF16),
            jax.ShapeDtypeStruct((n_tiles, N_HEADS * V_AUG_DIM, tm), _BF16),
            jax.ShapeDtypeStruct((seq, D_MODEL), _BF16),
            jax.ShapeDtypeStruct((seq, D_MODEL), _F32),
        ],
        scratch_shapes=[pltpu.VMEM((tm + MAX_WINDOW, D_MODEL), _F32)],
        compiler_params=pltpu.CompilerParams(
            dimension_semantics=("arbitrary",), vmem_limit_bytes=VMEM_LIMIT_BYTES),
        name="mixer_in",
    )(x2d, ln_g, w_in, w_grp, pool_scale, w_pp)


def _lambda_kernel(lq1_ref, lk1_ref, lq2_ref, lk2_ref, lam_ref):
    a = jnp.sum(lq1_ref[...] * lk1_ref[...], axis=-1, keepdims=True)
    b = jnp.sum(lq2_ref[...] * lk2_ref[...], axis=-1, keepdims=True)
    lam_ref[...] = jnp.exp(a) - jnp.exp(b) + LAMBDA_INIT


def _diff_attn_kernel(lam_ref, qT_ref, k_ref, vT_ref, bvec_ref, g_ref, o_ref,
                      qpad_ref, bias_ref, s_ref, p_ref, m_ref, acc_ref):
    i = pl.program_id(1)
    t = qT_ref.shape[2]
    n_vec = bvec_ref.shape[2]

    @pl.when(i == 0)
    def _():
        rows = 64
        for kind, c0 in ((0, 2 * t - 1), (1, t - 1)):
            for r0 in range(0, t, rows):
                x = jnp.broadcast_to(bvec_ref[0], (rows, n_vec))
                x = pltpu.roll(x, (n_vec - c0 + r0) % n_vec, 1, stride=1, stride_axis=0)[:, :t]
                if kind == 1:
                    key = r0 + lax.broadcasted_iota(jnp.int32, (rows, t), 0)
                    qry = lax.broadcasted_iota(jnp.int32, (rows, t), 1)
                    x = jnp.where(key // CHUNK <= qry // CHUNK, x, NEG_INF)
                bias_ref[kind, r0:r0 + rows, :] = x

    qT = qT_ref[0]
    row = lax.broadcasted_iota(jnp.int32, qT.shape, 0)
    zero = jnp.zeros_like(qT)
    qpad_ref[0] = jnp.where(row < HEAD_DIM, qT, zero)
    qpad_ref[1] = jnp.where(row >= HEAD_DIM, qT, zero)

    m_ref[...] = jnp.full(m_ref.shape, NEG_INF, _F32)
    acc_ref[...] = jnp.zeros(acc_ref.shape, _F32)
    p_ref[...] = jnp.zeros(p_ref.shape, _BF16)
    no_rescale = (jnp.ones((1, t), _F32),) * 2

    def scores(tile):
        kt = k_ref[tile]
        tile_max = []
        for c in range(2):
            s = jnp.dot(kt, qpad_ref[c], preferred_element_type=_F32)
            s_ref[c] = s
            tile_max.append(jnp.max(s, axis=0, keepdims=True))
        return tuple(tile_max)

    def softmax(bias_kind, tile_max):
        alphas = []
        for c in range(2):
            s = s_ref[c]
            if bias_kind is None:
                s_max = tile_max[c]
            else:
                s = s + bias_ref[bias_kind]
                s_max = jnp.max(s, axis=0, keepdims=True)
            m_old = m_ref[c]
            m_new = jnp.maximum(m_old, s_max)
            alpha = jnp.exp2(m_old - m_new)
            p = jnp.exp2(s - m_new)
            p_ref[c] = p.astype(_BF16)
            m_ref[c] = m_new
            alphas.append(alpha)
        return tuple(alphas)

    def accumulate(tile, alphas):
        vt = vT_ref[jnp.maximum(tile, 0)]
        for c in range(2):
            acc_ref[c] = alphas[c] * acc_ref[c] + jnp.dot(vt, p_ref[c],
                                                          preferred_element_type=_F32)

    n_far = jnp.maximum(i - 1, 0)
    first_max = scores(0)

    def far_step(tile, carry):
        pending, tile_max = carry
        accumulate(tile - 1, pending)
        pending = softmax(None, tile_max)
        return pending, scores(tile + 1)

    pending, _ = lax.fori_loop(0, n_far, far_step, (no_rescale, first_max))

    @pl.when(i == 0)
    def _():
        accumulate(0, softmax(1, None))

    @pl.when(i > 0)
    def _():
        accumulate(i - 2, pending)
        alphas = softmax(0, None)
        scores(i)
        accumulate(i - 1, alphas)
        accumulate(i, softmax(1, None))

    lam = lam_ref[0, 0]
    o = (acc_ref[0, :V_HEAD_DIM] / acc_ref[0, V_HEAD_DIM:V_HEAD_DIM + 1]
         - lam * (acc_ref[1, :V_HEAD_DIM] / acc_ref[1, V_HEAD_DIM:V_HEAD_DIM + 1]))
    ms = jnp.mean(o * o, axis=0, keepdims=True)
    o = o * lax.rsqrt(ms + SUBLN_EPS) * g_ref[...] * (1.0 - LAMBDA_INIT)
    o_ref[...] = o.T.astype(_BF16)


def _diff_attn(lam, qT, k, vT, bias_vec, subln_g_col):
    n_tiles, _, t = qT.shape
    seq = n_tiles * t
    n_vec = bias_vec.shape[2]
    return pl.pallas_call(
        _diff_attn_kernel,
        grid=(N_HEADS, n_tiles),
        in_specs=[
            pl.BlockSpec(memory_space=pltpu.SMEM),
            pl.BlockSpec((1, 2 * HEAD_DIM, t), lambda h, i: (i, h, 0)),
            pl.BlockSpec((n_tiles, t, 2 * HEAD_DIM), lambda h, i: (0, 0, h)),
            pl.BlockSpec((n_tiles, V_AUG_DIM, t), lambda h, i: (0, h, 0)),
            pl.BlockSpec((1, 1, n_vec), lambda h, i: (h, 0, 0)),
            pl.BlockSpec((V_HEAD_DIM, 1), lambda h, i: (0, 0)),
        ],
        out_specs=pl.BlockSpec((t, V_HEAD_DIM), lambda h, i: (i, h)),
        out_shape=jax.ShapeDtypeStruct((seq, ATTN_WIDTH), _BF16),
        scratch_shapes=[
            pltpu.VMEM((2, 2 * HEAD_DIM, t), _BF16),
            pltpu.VMEM((2, t, t), _F32),
            pltpu.VMEM((2, t, t), _F32),
            pltpu.VMEM((2, t, t), _BF16),
            pltpu.VMEM((2, 1, t), _F32),
            pltpu.VMEM((2, V_AUG_DIM, t), _F32),
        ],
        compiler_params=pltpu.CompilerParams(
            dimension_semantics=("arbitrary", "arbitrary"), vmem_limit_bytes=VMEM_LIMIT_BYTES),
        name="diff_attn",
    )(lam, qT, k, vT, bias_vec, subln_g_col)


def _t5_bucket(rel):
    nb = NUM_BUCKETS // 2
    max_exact = nb // 2
    bucket = (rel > 0).astype(jnp.int32) * nb
    n = jnp.abs(rel)
    n_f = jnp.maximum(n, 1).astype(jnp.float32)
    large = max_exact + (jnp.log(n_f / max_exact) / math.log(MAX_DISTANCE / max_exact)
                         * (nb - max_exact)).astype(jnp.int32)
    large = jnp.minimum(large, nb - 1)
    return bucket + jnp.where(n < max_exact, n, large)


def _near_bias_vector(rel_table, t):
    assert t >= MAX_DISTANCE and t % CHUNK == 0
    rel = jnp.arange(3 * t - 1, dtype=jnp.int32) - (2 * t - 1)
    far = rel_table[_t5_bucket(jnp.int32(-(t + 1)))].astype(_F32)
    vec = (rel_table[_t5_bucket(rel)].astype(_F32) - far) * LOG2_E
    vec = jnp.pad(vec[::-1], ((0, 1), (0, 0)))
    return vec.T.reshape(rel_table.shape[1], 1, 3 * t)


def _mixer_out_kernel(x_ref, on_ref, ga_ref, gp_ref, w_pa_ref, w_out_ref, g_mlp_ref,
                      w_up_ref, w_down_ref, g_fin_ref, out_ref):
    y_attn = jnp.dot(on_ref[...], w_pa_ref[...], preferred_element_type=_F32)
    merged = ga_ref[...].astype(_F32) * y_attn + gp_ref[...]
    x1 = x_ref[...] + jnp.dot(merged.astype(_BF16), w_out_ref[...], preferred_element_type=_F32)
    h = _rms_norm(x1, g_mlp_ref[...], NORM_EPS).astype(_BF16)
    x2 = x1
    for c in range(0, D_FF, FF_CHUNK):
        up = jnp.dot(h, w_up_ref[:, c:c + FF_CHUNK], preferred_element_type=_F32)
        act = jnp.square(jnp.maximum(up, 0.0)).astype(_BF16)
        x2 = x2 + jnp.dot(act, w_down_ref[c:c + FF_CHUNK, :], preferred_element_type=_F32)
    out_ref[...] = _rms_norm(x2, g_fin_ref[...], NORM_EPS)


def _mixer_out(x2d, on, ga, gp, w_pa, w_out, g_mlp, w_up, w_down, g_fin):
    seq = x2d.shape[0]
    tm = SEQ_TILE
    row_tile = pl.BlockSpec((tm, D_MODEL), lambda i: (i, 0))
    return pl.pallas_call(
        _mixer_out_kernel,
        grid=(seq // tm,),
        in_specs=[
            row_tile, row_tile, row_tile, row_tile,
            _resident((ATTN_WIDTH, D_MODEL)),
            _resident((D_MODEL, D_MODEL)),
            _resident((1, D_MODEL)),
            _resident((D_MODEL, D_FF)),
            _resident((D_FF, D_MODEL)),
            _resident((1, D_MODEL)),
        ],
        out_specs=row_tile,
        out_shape=jax.ShapeDtypeStruct((seq, D_MODEL), _F32),
        compiler_params=pltpu.CompilerParams(
            dimension_semantics=("arbitrary",), vmem_limit_bytes=VMEM_LIMIT_BYTES),
        name="mixer_out",
    )(x2d, on, ga, gp, w_pa, w_out, g_mlp, w_up, w_down, g_fin)


def kernel(x, ln_mix_g, w_in, rel_bias_table, lambda_q1, lambda_k1, lambda_q2, lambda_k2,
           subln_g, w_proj_attn, w_pool_grp, pool_scale, w_proj_pool, w_out,
           ln_mlp_g, w_mlp_up, w_mlp_down, ln_final_g):
    batch, seq, d_model = x.shape
    assert batch == 1 and d_model == D_MODEL and seq % SEQ_TILE == 0
    assert ln_mix_g.shape[0] == 1, "single-layer trunk"
    x2d = x.reshape(seq, d_model)

    qT, k, vT, ga, gp = _mixer_in(
        x2d, ln_mix_g, w_in[0].astype(_BF16), w_pool_grp[0].astype(_BF16), pool_scale,
        w_proj_pool[0].astype(_BF16))

    lam = pl.pallas_call(
        _lambda_kernel, out_shape=jax.ShapeDtypeStruct((1, 1), _F32), name="diff_lambda",
    )(lambda_q1, lambda_k1, lambda_q2, lambda_k2)
    bias_vec = _near_bias_vector(rel_bias_table, SEQ_TILE)
    on = _diff_attn(lam, qT, k, vT, bias_vec, subln_g.reshape(V_HEAD_DIM, 1))

    out = _mixer_out(
        x2d, on, ga, gp, w_proj_attn[0].astype(_BF16), w_out[0].astype(_BF16), ln_mlp_g,
        w_mlp_up[0].astype(_BF16), w_mlp_down[0].astype(_BF16), ln_final_g.reshape(1, d_model))
    return out.reshape(batch, seq, d_model)
```

```python
import functools
import math

import jax
import jax.numpy as jnp
from jax import lax
from jax.experimental import pallas as pl
from jax.experimental.pallas import tpu as pltpu

D_MODEL = 1024
N_HEADS = 8
HEAD_DIM = 64
V_HEAD_DIM = 2 * HEAD_DIM
QK_WIDTH = N_HEADS * 2 * HEAD_DIM
ATTN_WIDTH = N_HEADS * V_HEAD_DIM
BF16_SUBLANES = 16
V_AUG_DIM = V_HEAD_DIM + BF16_SUBLANES
POOL_WINDOWS = (2, 4, 8, 16)
POOL_GROUP_WIDTH = D_MODEL // len(POOL_WINDOWS)
MAX_WINDOW = max(POOL_WINDOWS)
D_FF = 4 * D_MODEL
CHUNK = 64
NUM_BUCKETS = 32
MAX_DISTANCE = 128
NORM_EPS = 1e-6
SUBLN_EPS = 1e-5
NEG_INF = -1e30
LAMBDA_INIT = 0.8 - 0.6 * math.exp(-0.3 * 0)
LOG2_E = math.log2(math.e)

SEQ_TILE = 512
FF_CHUNK = 1024
VMEM_LIMIT_BYTES = 56 * 1024 * 1024

_BF16 = jnp.bfloat16
_F32 = jnp.float32


def _rms_norm(x, g, eps):
    return x * lax.rsqrt(jnp.mean(x * x, axis=-1, keepdims=True) + eps) * g


def _resident(shape):
    return pl.BlockSpec(shape, lambda *_: (0,) * len(shape), pipeline_mode=pl.Buffered(1))


def _mixer_in_kernel(x_ref, g_ref, w_in_ref, w_grp_ref, pscale_ref, w_pp_ref,
                     qT_ref, k_ref, vT_ref, ga_ref, gp_ref, ubuf_ref):
    i = pl.program_id(0)
    tm = x_ref.shape[0]
    h = _rms_norm(x_ref[...], g_ref[...], NORM_EPS).astype(_BF16)

    def proj(col, width):
        return jnp.dot(h, w_in_ref[:, col:col + width], preferred_element_type=_F32)

    q = proj(0, QK_WIDTH) * (HEAD_DIM ** -0.5 * LOG2_E)
    qT_ref[0] = q.T.astype(_BF16)
    k_ref[0] = proj(QK_WIDTH, QK_WIDTH).astype(_BF16)
    vT = proj(2 * QK_WIDTH, ATTN_WIDTH).T.astype(_BF16)
    ones = jnp.ones((BF16_SUBLANES, tm), _BF16)
    for hd in range(N_HEADS):
        vT_ref[0, hd * V_AUG_DIM:hd * V_AUG_DIM + V_HEAD_DIM, :] = (
            vT[hd * V_HEAD_DIM:(hd + 1) * V_HEAD_DIM, :])
        vT_ref[0, hd * V_AUG_DIM + V_HEAD_DIM:(hd + 1) * V_AUG_DIM, :] = ones

    @pl.when(i == 0)
    def _():
        ubuf_ref[0:MAX_WINDOW, :] = jnp.zeros((MAX_WINDOW, D_MODEL), _F32)

    ubuf_ref[MAX_WINDOW:MAX_WINDOW + tm, :] = proj(2 * QK_WIDTH + ATTN_WIDTH, D_MODEL)
    pos = i * tm + lax.broadcasted_iota(jnp.int32, (tm, 1), 0)
    y_pool = jnp.zeros((tm, D_MODEL), _F32)
    for g, w in enumerate(POOL_WINDOWS):
        cols = slice(g * POOL_GROUP_WIDTH, (g + 1) * POOL_GROUP_WIDTH)
        u = ubuf_ref[MAX_WINDOW:MAX_WINDOW + tm, cols]
        window_sum = u
        for s in range(1, w):
            window_sum = window_sum + ubuf_ref[MAX_WINDOW - s:MAX_WINDOW - s + tm, cols]
        count = jnp.minimum(pos + 1, w).astype(_F32)
        z = window_sum / count - u
        zg = jnp.dot(z.astype(_BF16), w_grp_ref[g], preferred_element_type=_F32)
        zg = zg * pscale_ref[:, cols]
        y_pool = y_pool + jnp.dot(zg.astype(_BF16), w_pp_ref[cols, :],
                                  preferred_element_type=_F32)
    ubuf_ref[0:MAX_WINDOW, :] = ubuf_ref[tm:tm + MAX_WINDOW, :]

    gate_col = 2 * QK_WIDTH + ATTN_WIDTH + D_MODEL
    ga_ref[...] = jax.nn.sigmoid(proj(gate_col, D_MODEL)).astype(_BF16)
    gp_ref[...] = jax.nn.sigmoid(proj(gate_col + D_MODEL, D_MODEL)) * y_pool


def _mixer_in(x2d, ln_g, w_in, w_grp, pool_scale, w_pp):
    seq = x2d.shape[0]
    tm = SEQ_TILE
    n_tiles = seq // tm
    in_width = w_in.shape[1]
    row_tile = lambda i: (i, 0)
    tile3 = lambda i: (i, 0, 0)
    return pl.pallas_call(
        _mixer_in_kernel,
        grid=(n_tiles,),
        in_specs=[
            pl.BlockSpec((tm, D_MODEL), row_tile),
            _resident((1, D_MODEL)),
            _resident((D_MODEL, in_width)),
            _resident(w_grp.shape),
            _resident((1, D_MODEL)),
            _resident((D_MODEL, D_MODEL)),
        ],
        out_specs=[
            pl.BlockSpec((1, QK_WIDTH, tm), tile3),
            pl.BlockSpec((1, tm, QK_WIDTH), tile3),
            pl.BlockSpec((1, N_HEADS * V_AUG_DIM, tm), tile3),
            pl.BlockSpec((tm, D_MODEL), row_tile),
            pl.BlockSpec((tm, D_MODEL), row_tile),
        ],
        out_shape=[
            jax.ShapeDtypeStruct((n_tiles, QK_WIDTH, tm), _BF16),
            jax.ShapeDtypeStruct((n_tiles, tm, QK_WIDTH), _BF16),
            jax.ShapeDtypeStruct((n_tiles, N_HEADS * V_AUG_DIM, tm), _BF16),
            jax.ShapeDtypeStruct((seq, D_MODEL), _BF16),
            jax.ShapeDtypeStruct((seq, D_MODEL), _F32),
        ],
        scratch_shapes=[pltpu.VMEM((tm + MAX_WINDOW, D_MODEL), _F32)],
        compiler_params=pltpu.CompilerParams(
            dimension_semantics=("arbitrary",), vmem_limit_bytes=VMEM_LIMIT_BYTES),
        name="mixer_in",
    )(x2d, ln_g, w_in, w_grp, pool_scale, w_pp)


def _lambda_kernel(lq1_ref, lk1_ref, lq2_ref, lk2_ref, lam_ref):
    a = jnp.sum(lq1_ref[...] * lk1_ref[...], axis=-1, keepdims=True)
    b = jnp.sum(lq2_ref[...] * lk2_ref[...], axis=-1, keepdims=True)
    lam_ref[...] = jnp.exp(a) - jnp.exp(b) + LAMBDA_INIT


def _diff_attn_kernel(lam_ref, qT_ref, k_ref, vT_ref, bvec_ref, g_ref, o_ref,
                      qpad_ref, bias_ref, s_ref, m_ref, acc_ref):
    i = pl.program_id(1)
    t = qT_ref.shape[2]
    n_vec = bvec_ref.shape[2]

    @pl.when(i == 0)
    def _():
        rows = 64
        for kind, c0 in ((0, 2 * t - 1), (1, t - 1)):
            for r0 in range(0, t, rows):
                x = jnp.broadcast_to(bvec_ref[0], (rows, n_vec))
                x = pltpu.roll(x, (n_vec - c0 + r0) % n_vec, 1, stride=1, stride_axis=0)[:, :t]
                if kind == 1:
                    key = r0 + lax.broadcasted_iota(jnp.int32, (rows, t), 0)
                    qry = lax.broadcasted_iota(jnp.int32, (rows, t), 1)
                    x = jnp.where(key // CHUNK <= qry // CHUNK, x, NEG_INF)
                bias_ref[kind, r0:r0 + rows, :] = x

    qT = qT_ref[0]
    row = lax.broadcasted_iota(jnp.int32, qT.shape, 0)
    zero = jnp.zeros_like(qT)
    qpad_ref[0] = jnp.where(row < HEAD_DIM, qT, zero)
    qpad_ref[1] = jnp.where(row >= HEAD_DIM, qT, zero)

    m_ref[...] = jnp.full(m_ref.shape, NEG_INF, _F32)
    acc_ref[...] = jnp.zeros(acc_ref.shape, _F32)

    def scores(tile):
        kt = k_ref[tile]
        tile_max = []
        for c in range(2):
            s = jnp.dot(kt, qpad_ref[c], preferred_element_type=_F32)
            s_ref[c] = s
            tile_max.append(jnp.max(s, axis=0, keepdims=True))
        return tuple(tile_max)

    def attend(tile, bias_kind, tile_max):
        vt = vT_ref[tile]
        for c in range(2):
            s = s_ref[c]
            if bias_kind is None:
                s_max = tile_max[c]
            else:
                s = s + bias_ref[bias_kind]
                s_max = jnp.max(s, axis=0, keepdims=True)
            m_old = m_ref[c]
            m_new = jnp.maximum(m_old, s_max)
            alpha = jnp.exp2(m_old - m_new)
            p = jnp.exp2(s - m_new).astype(_BF16)
            acc_ref[c] = alpha * acc_ref[c] + jnp.dot(vt, p, preferred_element_type=_F32)
            m_ref[c] = m_new

    n_far = jnp.maximum(i - 1, 0)

    def far_step(tile, tile_max):
        attend(tile, None, tile_max)
        return scores(tile + 1)

    lax.fori_loop(0, n_far, far_step, scores(0))

    @pl.when(i > 0)
    def _():
        attend(i - 1, 0, None)
        scores(i)

    attend(i, 1, None)

    lam = lam_ref[0, 0]
    o = (acc_ref[0, :V_HEAD_DIM] / acc_ref[0, V_HEAD_DIM:V_HEAD_DIM + 1]
         - lam * (acc_ref[1, :V_HEAD_DIM] / acc_ref[1, V_HEAD_DIM:V_HEAD_DIM + 1]))
    ms = jnp.mean(o * o, axis=0, keepdims=True)
    o = o * lax.rsqrt(ms + SUBLN_EPS) * g_ref[...] * (1.0 - LAMBDA_INIT)
    o_ref[...] = o.T.astype(_BF16)


def _diff_attn(lam, qT, k, vT, bias_vec, subln_g_col):
    n_tiles, _, t = qT.shape
    seq = n_tiles * t
    n_vec = bias_vec.shape[2]
    return pl.pallas_call(
        _diff_attn_kernel,
        grid=(N_HEADS, n_tiles),
        in_specs=[
            pl.BlockSpec(memory_space=pltpu.SMEM),
            pl.BlockSpec((1, 2 * HEAD_DIM, t), lambda h, i: (i, h, 0)),
            pl.BlockSpec((n_tiles, t, 2 * HEAD_DIM), lambda h, i: (0, 0, h)),
            pl.BlockSpec((n_tiles, V_AUG_DIM, t), lambda h, i: (0, h, 0)),
            pl.BlockSpec((1, 1, n_vec), lambda h, i: (h, 0, 0)),
            pl.BlockSpec((V_HEAD_DIM, 1), lambda h, i: (0, 0)),
        ],
        out_specs=pl.BlockSpec((t, V_HEAD_DIM), lambda h, i: (i, h)),
        out_shape=jax.ShapeDtypeStruct((seq, ATTN_WIDTH), _BF16),
        scratch_shapes=[
            pltpu.VMEM((2, 2 * HEAD_DIM, t), _BF16),
            pltpu.VMEM((2, t, t), _F32),
            pltpu.VMEM((2, t, t), _F32),
            pltpu.VMEM((2, 1, t), _F32),
            pltpu.VMEM((2, V_AUG_DIM, t), _F32),
        ],
        compiler_params=pltpu.CompilerParams(
            dimension_semantics=("arbitrary", "arbitrary"), vmem_limit_bytes=VMEM_LIMIT_BYTES),
        name="diff_attn",
    )(lam, qT, k, vT, bias_vec, subln_g_col)


def _t5_bucket(rel):
    nb = NUM_BUCKETS // 2
    max_exact = nb // 2
    bucket = (rel > 0).astype(jnp.int32) * nb
    n = jnp.abs(rel)
    n_f = jnp.maximum(n, 1).astype(jnp.float32)
    large = max_exact + (jnp.log(n_f / max_exact) / math.log(MAX_DISTANCE / max_exact)
                         * (nb - max_exact)).astype(jnp.int32)
    large = jnp.minimum(large, nb - 1)
    return bucket + jnp.where(n < max_exact, n, large)


def _near_bias_vector(rel_table, t):
    assert t >= MAX_DISTANCE and t % CHUNK == 0
    rel = jnp.arange(3 * t - 1, dtype=jnp.int32) - (2 * t - 1)
    far = rel_table[_t5_bucket(jnp.int32(-(t + 1)))].astype(_F32)
    vec = (rel_table[_t5_bucket(rel)].astype(_F32) - far) * LOG2_E
    vec = jnp.pad(vec[::-1], ((0, 1), (0, 0)))
    return vec.T.reshape(rel_table.shape[1], 1, 3 * t)


def _mixer_out_kernel(x_ref, on_ref, ga_ref, gp_ref, w_pa_ref, w_out_ref, g_mlp_ref,
                      w_up_ref, w_down_ref, g_fin_ref, out_ref):
    y_attn = jnp.dot(on_ref[...], w_pa_ref[...], preferred_element_type=_F32)
    merged = ga_ref[...].astype(_F32) * y_attn + gp_ref[...]
    x1 = x_ref[...] + jnp.dot(merged.astype(_BF16), w_out_ref[...], preferred_element_type=_F32)
    h = _rms_norm(x1, g_mlp_ref[...], NORM_EPS).astype(_BF16)
    x2 = x1
    for c in range(0, D_FF, FF_CHUNK):
        up = jnp.dot(h, w_up_ref[:, c:c + FF_CHUNK], preferred_element_type=_F32)
        act = jnp.square(jnp.maximum(up, 0.0)).astype(_BF16)
        x2 = x2 + jnp.dot(act, w_down_ref[c:c + FF_CHUNK, :], preferred_element_type=_F32)
    out_ref[...] = _rms_norm(x2, g_fin_ref[...], NORM_EPS)


def _mixer_out(x2d, on, ga, gp, w_pa, w_out, g_mlp, w_up, w_down, g_fin):
    seq = x2d.shape[0]
    tm = SEQ_TILE
    row_tile = pl.BlockSpec((tm, D_MODEL), lambda i: (i, 0))
    return pl.pallas_call(
        _mixer_out_kernel,
        grid=(seq // tm,),
        in_specs=[
            row_tile, row_tile, row_tile, row_tile,
            _resident((ATTN_WIDTH, D_MODEL)),
            _resident((D_MODEL, D_MODEL)),
            _resident((1, D_MODEL)),
            _resident((D_MODEL, D_FF)),
            _resident((D_FF, D_MODEL)),
            _resident((1, D_MODEL)),
        ],
        out_specs=row_tile,
        out_shape=jax.ShapeDtypeStruct((seq, D_MODEL), _F32),
        compiler_params=pltpu.CompilerParams(
            dimension_semantics=("arbitrary",), vmem_limit_bytes=VMEM_LIMIT_BYTES),
        name="mixer_out",
    )(x2d, on, ga, gp, w_pa, w_out, g_mlp, w_up, w_down, g_fin)


def kernel(x, ln_mix_g, w_in, rel_bias_table, lambda_q1, lambda_k1, lambda_q2, lambda_k2,
           subln_g, w_proj_attn, w_pool_grp, pool_scale, w_proj_pool, w_out,
           ln_mlp_g, w_mlp_up, w_mlp_down, ln_final_g):
    batch, seq, d_model = x.shape
    assert batch == 1 and d_model == D_MODEL and seq % SEQ_TILE == 0
    assert ln_mix_g.shape[0] == 1, "single-layer trunk"
    x2d = x.reshape(seq, d_model)

    qT, k, vT, ga, gp = _mixer_in(
        x2d, ln_mix_g, w_in[0].astype(_BF16), w_pool_grp[0].astype(_BF16), pool_scale,
        w_proj_pool[0].astype(_BF16))

    lam = pl.pallas_call(
        _lambda_kernel, out_shape=jax.ShapeDtypeStruct((1, 1), _F32), name="diff_lambda",
    )(lambda_q1, lambda_k1, lambda_q2, lambda_k2)
    bias_vec = _near_bias_vector(rel_bias_table, SEQ_TILE)
    on = _diff_attn(lam, qT, k, vT, bias_vec, subln_g.reshape(V_HEAD_DIM, 1))

    out = _mixer_out(
        x2d, on, ga, gp, w_proj_attn[0].astype(_BF16), w_out[0].astype(_BF16), ln_mlp_g,
        w_mlp_up[0].astype(_BF16), w_mlp_down[0].astype(_BF16), ln_final_g.reshape(1, d_model))
    return out.reshape(batch, seq, d_model)
```

```python
import functools
import math

import jax
import jax.numpy as jnp
from jax import lax
from jax.experimental import pallas as pl
from jax.experimental.pallas import tpu as pltpu

D_MODEL = 1024
N_HEADS = 8
HEAD_DIM = 64
V_HEAD_DIM = 2 * HEAD_DIM
QK_WIDTH = N_HEADS * 2 * HEAD_DIM
ATTN_WIDTH = N_HEADS * V_HEAD_DIM
BF16_SUBLANES = 16
V_AUG_DIM = V_HEAD_DIM + BF16_SUBLANES
POOL_WINDOWS = (2, 4, 8, 16)
POOL_GROUP_WIDTH = D_MODEL // len(POOL_WINDOWS)
MAX_WINDOW = max(POOL_WINDOWS)
D_FF = 4 * D_MODEL
CHUNK = 64
NUM_BUCKETS = 32
MAX_DISTANCE = 128
NORM_EPS = 1e-6
SUBLN_EPS = 1e-5
NEG_INF = -1e30
LAMBDA_INIT = 0.8 - 0.6 * math.exp(-0.3 * 0)
LOG2_E = math.log2(math.e)

SEQ_TILE = 512
FF_CHUNK = 1024
VMEM_LIMIT_BYTES = 56 * 1024 * 1024

_BF16 = jnp.bfloat16
_F32 = jnp.float32


def _rms_norm(x, g, eps):
    return x * lax.rsqrt(jnp.mean(x * x, axis=-1, keepdims=True) + eps) * g


def _resident(shape):
    return pl.BlockSpec(shape, lambda *_: (0,) * len(shape), pipeline_mode=pl.Buffered(1))


def _mixer_in_kernel(x_ref, g_ref, w_in_ref, w_grp_ref, pscale_ref, w_pp_ref,
                     qT_ref, k_ref, vT_ref, ga_ref, gp_ref, ubuf_ref):
    i = pl.program_id(0)
    tm = x_ref.shape[0]
    h = _rms_norm(x_ref[...], g_ref[...], NORM_EPS).astype(_BF16)

    def proj(col, width):
        return jnp.dot(h, w_in_ref[:, col:col + width], preferred_element_type=_F32)

    q = proj(0, QK_WIDTH) * (HEAD_DIM ** -0.5 * LOG2_E)
    qT_ref[0] = q.T.astype(_BF16)
    k_ref[0] = proj(QK_WIDTH, QK_WIDTH).astype(_BF16)
    vT = proj(2 * QK_WIDTH, ATTN_WIDTH).T.astype(_BF16)
    ones = jnp.ones((BF16_SUBLANES, tm), _BF16)
    for hd in range(N_HEADS):
        vT_ref[0, hd * V_AUG_DIM:hd * V_AUG_DIM + V_HEAD_DIM, :] = (
            vT[hd * V_HEAD_DIM:(hd + 1) * V_HEAD_DIM, :])
        vT_ref[0, hd * V_AUG_DIM + V_HEAD_DIM:(hd + 1) * V_AUG_DIM, :] = ones

    @pl.when(i == 0)
    def _():
        ubuf_ref[0:MAX_WINDOW, :] = jnp.zeros((MAX_WINDOW, D_MODEL), _F32)

    ubuf_ref[MAX_WINDOW:MAX_WINDOW + tm, :] = proj(2 * QK_WIDTH + ATTN_WIDTH, D_MODEL)
    pos = i * tm + lax.broadcasted_iota(jnp.int32, (tm, 1), 0)
    y_pool = jnp.zeros((tm, D_MODEL), _F32)
    for g, w in enumerate(POOL_WINDOWS):
        cols = slice(g * POOL_GROUP_WIDTH, (g + 1) * POOL_GROUP_WIDTH)
        u = ubuf_ref[MAX_WINDOW:MAX_WINDOW + tm, cols]
        window_sum = u
        for s in range(1, w):
            window_sum = window_sum + ubuf_ref[MAX_WINDOW - s:MAX_WINDOW - s + tm, cols]
        count = jnp.minimum(pos + 1, w).astype(_F32)
        z = window_sum / count - u
        zg = jnp.dot(z.astype(_BF16), w_grp_ref[g], preferred_element_type=_F32)
        zg = zg * pscale_ref[:, cols]
        y_pool = y_pool + jnp.dot(zg.astype(_BF16), w_pp_ref[cols, :],
                                  preferred_element_type=_F32)
    ubuf_ref[0:MAX_WINDOW, :] = ubuf_ref[tm:tm + MAX_WINDOW, :]

    gate_col = 2 * QK_WIDTH + ATTN_WIDTH + D_MODEL
    ga_ref[...] = jax.nn.sigmoid(proj(gate_col, D_MODEL)).astype(_BF16)
    gp_ref[...] = jax.nn.sigmoid(proj(gate_col + D_MODEL, D_MODEL)) * y_pool


def _mixer_in(x2d, ln_g, w_in, w_grp, pool_scale, w_pp):
    seq = x2d.shape[0]
    tm = SEQ_TILE
    n_tiles = seq // tm
    in_width = w_in.shape[1]
    row_tile = lambda i: (i, 0)
    tile3 = lambda i: (i, 0, 0)
    return pl.pallas_call(
        _mixer_in_kernel,
        grid=(n_tiles,),
        in_specs=[
            pl.BlockSpec((tm, D_MODEL), row_tile),
            _resident((1, D_MODEL)),
            _resident((D_MODEL, in_width)),
            _resident(w_grp.shape),
            _resident((1, D_MODEL)),
            _resident((D_MODEL, D_MODEL)),
        ],
        out_specs=[
            pl.BlockSpec((1, QK_WIDTH, tm), tile3),
            pl.BlockSpec((1, tm, QK_WIDTH), tile3),
            pl.BlockSpec((1, N_HEADS * V_AUG_DIM, tm), tile3),
            pl.BlockSpec((tm, D_MODEL), row_tile),
            pl.BlockSpec((tm, D_MODEL), row_tile),
        ],
        out_shape=[
            jax.ShapeDtypeStruct((n_tiles, QK_WIDTH, tm), _BF16),
            jax.ShapeDtypeStruct((n_tiles, tm, QK_WIDTH), _BF16),
            jax.ShapeDtypeStruct((n_tiles, N_HEADS * V_AUG_DIM, tm), _BF16),
            jax.ShapeDtypeStruct((seq, D_MODEL), _BF16),
            jax.ShapeDtypeStruct((seq, D_MODEL), _F32),
        ],
        scratch_shapes=[pltpu.VMEM((tm + MAX_WINDOW, D_MODEL), _F32)],
        compiler_params=pltpu.CompilerParams(
            dimension_semantics=("arbitrary",), vmem_limit_bytes=VMEM_LIMIT_BYTES),
        name="mixer_in",
    )(x2d, ln_g, w_in, w_grp, pool_scale, w_pp)


def _lambda_kernel(lq1_ref, lk1_ref, lq2_ref, lk2_ref, lam_ref):
    a = jnp.sum(lq1_ref[...] * lk1_ref[...], axis=-1, keepdims=True)
    b = jnp.sum(lq2_ref[...] * lk2_ref[...], axis=-1, keepdims=True)
    lam_ref[...] = jnp.exp(a) - jnp.exp(b) + LAMBDA_INIT


def _diff_attn_kernel(lam_ref, qT_ref, k_ref, vT_ref, bvec_ref, g_ref, o_ref,
                      qpad_ref, bias_ref, s_ref, p_ref, m_ref, acc_ref):
    i = pl.program_id(1)
    t = qT_ref.shape[2]
    n_vec = bvec_ref.shape[2]

    @pl.when(i == 0)
    def _():
        rows = 64
        for kind, c0 in ((0, 2 * t - 1), (1, t - 1)):
            for r0 in range(0, t, rows):
                x = jnp.broadcast_to(bvec_ref[0], (rows, n_vec))
                x = pltpu.roll(x, (n_vec - c0 + r0) % n_vec, 1, stride=1, stride_axis=0)[:, :t]
                if kind == 1:
                    key = r0 + lax.broadcasted_iota(jnp.int32, (rows, t), 0)
                    qry = lax.broadcasted_iota(jnp.int32, (rows, t), 1)
                    x = jnp.where(key // CHUNK <= qry // CHUNK, x, NEG_INF)
                bias_ref[kind, r0:r0 + rows, :] = x

    qT = qT_ref[0]
    row = lax.broadcasted_iota(jnp.int32, qT.shape, 0)
    zero = jnp.zeros_like(qT)
    qpad_ref[0] = jnp.where(row < HEAD_DIM, qT, zero)
    qpad_ref[1] = jnp.where(row >= HEAD_DIM, qT, zero)

    m_ref[...] = jnp.full(m_ref.shape, NEG_INF, _F32)
    acc_ref[...] = jnp.zeros(acc_ref.shape, _F32)
    p_ref[...] = jnp.zeros(p_ref.shape, _BF16)
    no_rescale = (jnp.ones((1, t), _F32),) * 2

    def scores(tile):
        kt = k_ref[tile]
        tile_max = []
        for c in range(2):
            s = jnp.dot(kt, qpad_ref[c], preferred_element_type=_F32)
            s_ref[c] = s
            tile_max.append(jnp.max(s, axis=0, keepdims=True))
        return tuple(tile_max)

    def softmax(bias_kind, tile_max):
        alphas = []
        for c in range(2):
            s = s_ref[c]
            if bias_kind is None:
                s_max = tile_max[c]
            else:
                s = s + bias_ref[bias_kind]
                s_max = jnp.max(s, axis=0, keepdims=True)
            m_old = m_ref[c]
            m_new = jnp.maximum(m_old, s_max)
            alpha = jnp.exp2(m_old - m_new)
            p = jnp.exp2(s - m_new)
            p_ref[c] = p.astype(_BF16)
            m_ref[c] = m_new
            alphas.append(alpha)
        return tuple(alphas)

    def accumulate(tile, alphas):
        vt = vT_ref[jnp.maximum(tile, 0)]
        for c in range(2):
            acc_ref[c] = alphas[c] * acc_ref[c] + jnp.dot(vt, p_ref[c],
                                                          preferred_element_type=_F32)

    n_far = jnp.maximum(i - 1, 0)
    first_max = scores(0)

    def far_step(tile, carry):
        pending, tile_max = carry
        accumulate(tile - 1, pending)
        pending = softmax(None, tile_max)
        return pending, scores(tile + 1)

    def far_pair(jj, carry):
        return far_step(2 * jj + 1, far_step(2 * jj, carry))

    carry = lax.fori_loop(0, n_far // 2, far_pair, (no_rescale, first_max))
    pending, _ = lax.cond(n_far % 2 == 1, lambda c: far_step(n_far - 1, c), lambda c: c, carry)

    @pl.when(i == 0)
    def _():
        accumulate(0, softmax(1, None))

    @pl.when(i > 0)
    def _():
        accumulate(i - 2, pending)
        alphas = softmax(0, None)
        scores(i)
        accumulate(i - 1, alphas)
        accumulate(i, softmax(1, None))

    lam = lam_ref[0, 0]
    o = (acc_ref[0, :V_HEAD_DIM] / acc_ref[0, V_HEAD_DIM:V_HEAD_DIM + 1]
         - lam * (acc_ref[1, :V_HEAD_DIM] / acc_ref[1, V_HEAD_DIM:V_HEAD_DIM + 1]))
    ms = jnp.mean(o * o, axis=0, keepdims=True)
    o = o * lax.rsqrt(ms + SUBLN_EPS) * g_ref[...] * (1.0 - LAMBDA_INIT)
    o_ref[...] = o.T.astype(_BF16)


def _diff_attn(lam, qT, k, vT, bias_vec, subln_g_col):
    n_tiles, _, t = qT.shape
    seq = n_tiles * t
    n_vec = bias_vec.shape[2]
    return pl.pallas_call(
        _diff_attn_kernel,
        grid=(N_HEADS, n_tiles),
        in_specs=[
            pl.BlockSpec(memory_space=pltpu.SMEM),
            pl.BlockSpec((1, 2 * HEAD_DIM, t), lambda h, i: (i, h, 0)),
            pl.BlockSpec((n_tiles, t, 2 * HEAD_DIM), lambda h, i: (0, 0, h)),
            pl.BlockSpec((n_tiles, V_AUG_DIM, t), lambda h, i: (0, h, 0)),
            pl.BlockSpec((1, 1, n_vec), lambda h, i: (h, 0, 0)),
            pl.BlockSpec((V_HEAD_DIM, 1), lambda h, i: (0, 0)),
        ],
        out_specs=pl.BlockSpec((t, V_HEAD_DIM), lambda h, i: (i, h)),
        out_shape=jax.ShapeDtypeStruct((seq, ATTN_WIDTH), _BF16),
        scratch_shapes=[
            pltpu.VMEM((2, 2 * HEAD_DIM, t), _BF16),
            pltpu.VMEM((2, t, t), _F32),
            pltpu.VMEM((2, t, t), _F32),
            pltpu.VMEM((2, t, t), _BF16),
            pltpu.VMEM((2, 1, t), _F32),
            pltpu.VMEM((2, V_AUG_DIM, t), _F32),
        ],
        compiler_params=pltpu.CompilerParams(
            dimension_semantics=("arbitrary", "arbitrary"), vmem_limit_bytes=VMEM_LIMIT_BYTES),
        name="diff_attn",
    )(lam, qT, k, vT, bias_vec, subln_g_col)


def _t5_bucket(rel):
    nb = NUM_BUCKETS // 2
    max_exact = nb // 2
    bucket = (rel > 0).astype(jnp.int32) * nb
    n = jnp.abs(rel)
    n_f = jnp.maximum(n, 1).astype(jnp.float32)
    large = max_exact + (jnp.log(n_f / max_exact) / math.log(MAX_DISTANCE / max_exact)
                         * (nb - max_exact)).astype(jnp.int32)
    large = jnp.minimum(large, nb - 1)
    return bucket + jnp.where(n < max_exact, n, large)


def _near_bias_vector(rel_table, t):
    assert t >= MAX_DISTANCE and t % CHUNK == 0
    rel = jnp.arange(3 * t - 1, dtype=jnp.int32) - (2 * t - 1)
    far = rel_table[_t5_bucket(jnp.int32(-(t + 1)))].astype(_F32)
    vec = (rel_table[_t5_bucket(rel)].astype(_F32) - far) * LOG2_E
    vec = jnp.pad(vec[::-1], ((0, 1), (0, 0)))
    return vec.T.reshape(rel_table.shape[1], 1, 3 * t)


def _mixer_out_kernel(x_ref, on_ref, ga_ref, gp_ref, w_pa_ref, w_out_ref, g_mlp_ref,
                      w_up_ref, w_down_ref, g_fin_ref, out_ref):
    y_attn = jnp.dot(on_ref[...], w_pa_ref[...], preferred_element_type=_F32)
    merged = ga_ref[...].astype(_F32) * y_attn + gp_ref[...]
    x1 = x_ref[...] + jnp.dot(merged.astype(_BF16), w_out_ref[...], preferred_element_type=_F32)
    h = _rms_norm(x1, g_mlp_ref[...], NORM_EPS).astype(_BF16)
    x2 = x1
    for c in range(0, D_FF, FF_CHUNK):
        up = jnp.dot(h, w_up_ref[:, c:c + FF_CHUNK], preferred_element_type=_F32)
        act = jnp.square(jnp.maximum(up, 0.0)).astype(_BF16)
        x2 = x2 + jnp.dot(act, w_down_ref[c:c + FF_CHUNK, :], preferred_element_type=_F32)
    out_ref[...] = _rms_norm(x2, g_fin_ref[...], NORM_EPS)


def _mixer_out(x2d, on, ga, gp, w_pa, w_out, g_mlp, w_up, w_down, g_fin):
    seq = x2d.shape[0]
    tm = SEQ_TILE
    row_tile = pl.BlockSpec((tm, D_MODEL), lambda i: (i, 0))
    return pl.pallas_call(
        _mixer_out_kernel,
        grid=(seq // tm,),
        in_specs=[
            row_tile, row_tile, row_tile, row_tile,
            _resident((ATTN_WIDTH, D_MODEL)),
            _resident((D_MODEL, D_MODEL)),
            _resident((1, D_MODEL)),
            _resident((D_MODEL, D_FF)),
            _resident((D_FF, D_MODEL)),
            _resident((1, D_MODEL)),
        ],
        out_specs=row_tile,
        out_shape=jax.ShapeDtypeStruct((seq, D_MODEL), _F32),
        compiler_params=pltpu.CompilerParams(
            dimension_semantics=("arbitrary",), vmem_limit_bytes=VMEM_LIMIT_BYTES),
        name="mixer_out",
    )(x2d, on, ga, gp, w_pa, w_out, g_mlp, w_up, w_down, g_fin)


def kernel(x, ln_mix_g, w_in, rel_bias_table, lambda_q1, lambda_k1, lambda_q2, lambda_k2,
           subln_g, w_proj_attn, w_pool_grp, pool_scale, w_proj_pool, w_out,
           ln_mlp_g, w_mlp_up, w_mlp_down, ln_final_g):
    batch, seq, d_model = x.shape
    assert batch == 1 and d_model == D_MODEL and seq % SEQ_TILE == 0
    assert ln_mix_g.shape[0] == 1, "single-layer trunk"
    x2d = x.reshape(seq, d_model)

    qT, k, vT, ga, gp = _mixer_in(
        x2d, ln_mix_g, w_in[0].astype(_BF16), w_pool_grp[0].astype(_BF16), pool_scale,
        w_proj_pool[0].astype(_BF16))

    lam = pl.pallas_call(
        _lambda_kernel, out_shape=jax.ShapeDtypeStruct((1, 1), _F32), name="diff_lambda",
    )(lambda_q1, lambda_k1, lambda_q2, lambda_k2)
    bias_vec = _near_bias_vector(rel_bias_table, SEQ_TILE)
    on = _diff_attn(lam, qT, k, vT, bias_vec, subln_g.reshape(V_HEAD_DIM, 1))

    out = _mixer_out(
        x2d, on, ga, gp, w_proj_attn[0].astype(_BF16), w_out[0].astype(_BF16), ln_mlp_g,
        w_mlp_up[0].astype(_BF16), w_mlp_down[0].astype(_BF16), ln_final_g.reshape(1, d_model))
    return out.reshape(batch, seq, d_model)
```

```python
import functools
import math

import jax
import jax.numpy as jnp
from jax import lax
from jax.experimental import pallas as pl
from jax.experimental.pallas import tpu as pltpu

D_MODEL = 1024
N_HEADS = 8
HEAD_DIM = 64
V_HEAD_DIM = 2 * HEAD_DIM
QK_WIDTH = N_HEADS * 2 * HEAD_DIM
ATTN_WIDTH = N_HEADS * V_HEAD_DIM
BF16_SUBLANES = 16
V_AUG_DIM = V_HEAD_DIM + BF16_SUBLANES
POOL_WINDOWS = (2, 4, 8, 16)
POOL_GROUP_WIDTH = D_MODEL // len(POOL_WINDOWS)
MAX_WINDOW = max(POOL_WINDOWS)
D_FF = 4 * D_MODEL
CHUNK = 64
NUM_BUCKETS = 32
MAX_DISTANCE = 128
NORM_EPS = 1e-6
SUBLN_EPS = 1e-5
NEG_INF = -1e30
LAMBDA_INIT = 0.8 - 0.6 * math.exp(-0.3 * 0)
LOG2_E = math.log2(math.e)

SEQ_TILE = 512
FF_CHUNK = 1024
VMEM_LIMIT_BYTES = 56 * 1024 * 1024

_BF16 = jnp.bfloat16
_F32 = jnp.float32


def _rms_norm(x, g, eps):
    return x * lax.rsqrt(jnp.mean(x * x, axis=-1, keepdims=True) + eps) * g


def _resident(shape):
    return pl.BlockSpec(shape, lambda *_: (0,) * len(shape), pipeline_mode=pl.Buffered(1))


def _mixer_in_kernel(x_ref, g_ref, w_in_ref, w_grp_ref, pscale_ref, w_pp_ref,
                     qT_ref, k_ref, vT_ref, ga_ref, gp_ref, ubuf_ref):
    i = pl.program_id(0)
    tm = x_ref.shape[0]
    h = _rms_norm(x_ref[...], g_ref[...], NORM_EPS).astype(_BF16)

    def proj(col, width):
        return jnp.dot(h, w_in_ref[:, col:col + width], preferred_element_type=_F32)

    q = proj(0, QK_WIDTH) * (HEAD_DIM ** -0.5 * LOG2_E)
    qT_ref[0] = q.T.astype(_BF16)
    k_ref[0] = proj(QK_WIDTH, QK_WIDTH).astype(_BF16)
    vT = proj(2 * QK_WIDTH, ATTN_WIDTH).T.astype(_BF16)
    ones = jnp.ones((BF16_SUBLANES, tm), _BF16)
    for hd in range(N_HEADS):
        vT_ref[0, hd * V_AUG_DIM:hd * V_AUG_DIM + V_HEAD_DIM, :] = (
            vT[hd * V_HEAD_DIM:(hd + 1) * V_HEAD_DIM, :])
        vT_ref[0, hd * V_AUG_DIM + V_HEAD_DIM:(hd + 1) * V_AUG_DIM, :] = ones

    @pl.when(i == 0)
    def _():
        ubuf_ref[0:MAX_WINDOW, :] = jnp.zeros((MAX_WINDOW, D_MODEL), _F32)

    ubuf_ref[MAX_WINDOW:MAX_WINDOW + tm, :] = proj(2 * QK_WIDTH + ATTN_WIDTH, D_MODEL)
    pos = i * tm + lax.broadcasted_iota(jnp.int32, (tm, 1), 0)
    y_pool = jnp.zeros((tm, D_MODEL), _F32)
    for g, w in enumerate(POOL_WINDOWS):
        cols = slice(g * POOL_GROUP_WIDTH, (g + 1) * POOL_GROUP_WIDTH)
        u = ubuf_ref[MAX_WINDOW:MAX_WINDOW + tm, cols]
        window_sum = u
        for s in range(1, w):
            window_sum = window_sum + ubuf_ref[MAX_WINDOW - s:MAX_WINDOW - s + tm, cols]
        count = jnp.minimum(pos + 1, w).astype(_F32)
        z = window_sum / count - u
        zg = jnp.dot(z.astype(_BF16), w_grp_ref[g], preferred_element_type=_F32)
        zg = zg * pscale_ref[:, cols]
        y_pool = y_pool + jnp.dot(zg.astype(_BF16), w_pp_ref[cols, :],
                                  preferred_element_type=_F32)
    ubuf_ref[0:MAX_WINDOW, :] = ubuf_ref[tm:tm + MAX_WINDOW, :]

    gate_col = 2 * QK_WIDTH + ATTN_WIDTH + D_MODEL
    ga_ref[...] = jax.nn.sigmoid(proj(gate_col, D_MODEL)).astype(_BF16)
    gp_ref[...] = jax.nn.sigmoid(proj(gate_col + D_MODEL, D_MODEL)) * y_pool


def _mixer_in(x2d, ln_g, w_in, w_grp, pool_scale, w_pp):
    seq = x2d.shape[0]
    tm = SEQ_TILE
    n_tiles = seq // tm
    in_width = w_in.shape[1]
    row_tile = lambda i: (i, 0)
    tile3 = lambda i: (i, 0, 0)
    return pl.pallas_call(
        _mixer_in_kernel,
        grid=(n_tiles,),
        in_specs=[
            pl.BlockSpec((tm, D_MODEL), row_tile),
            _resident((1, D_MODEL)),
            _resident((D_MODEL, in_width)),
            _resident(w_grp.shape),
            _resident((1, D_MODEL)),
            _resident((D_MODEL, D_MODEL)),
        ],
        out_specs=[
            pl.BlockSpec((1, QK_WIDTH, tm), tile3),
            pl.BlockSpec((1, tm, QK_WIDTH), tile3),
            pl.BlockSpec((1, N_HEADS * V_AUG_DIM, tm), tile3),
            pl.BlockSpec((tm, D_MODEL), row_tile),
            pl.BlockSpec((tm, D_MODEL), row_tile),
        ],
        out_shape=[
            jax.ShapeDtypeStruct((n_tiles, QK_WIDTH, tm), _BF16),
            jax.ShapeDtypeStruct((n_tiles, tm, QK_WIDTH), _BF16),
            jax.ShapeDtypeStruct((n_tiles, N_HEADS * V_AUG_DIM, tm), _BF16),
            jax.ShapeDtypeStruct((seq, D_MODEL), _BF16),
            jax.ShapeDtypeStruct((seq, D_MODEL), _F32),
        ],
        scratch_shapes=[pltpu.VMEM((tm + MAX_WINDOW, D_MODEL), _F32)],
        compiler_params=pltpu.CompilerParams(
            dimension_semantics=("arbitrary",), vmem_limit_bytes=VMEM_LIMIT_BYTES),
        name="mixer_in",
    )(x2d, ln_g, w_in, w_grp, pool_scale, w_pp)


def _lambda_kernel(lq1_ref, lk1_ref, lq2_ref, lk2_ref, lam_ref):
    a = jnp.sum(lq1_ref[...] * lk1_ref[...], axis=-1, keepdims=True)
    b = jnp.sum(lq2_ref[...] * lk2_ref[...], axis=-1, keepdims=True)
    lam_ref[...] = jnp.exp(a) - jnp.exp(b) + LAMBDA_INIT


def _diff_attn_kernel(lam_ref, qT_ref, qT_next_ref, k_ref, vT_ref, bvec_ref, g_ref, o_ref,
                      qpad_ref, qpad_next_ref, bias_ref, s_ref, p_ref, m_ref, acc_ref,
                      first_max_ref):
    i = pl.program_id(1)
    t = qT_ref.shape[2]
    n_vec = bvec_ref.shape[2]

    @pl.when(i == 0)
    def _():
        rows = 64
        for kind, c0 in ((0, 2 * t - 1), (1, t - 1)):
            for r0 in range(0, t, rows):
                x = jnp.broadcast_to(bvec_ref[0], (rows, n_vec))
                x = pltpu.roll(x, (n_vec - c0 + r0) % n_vec, 1, stride=1, stride_axis=0)[:, :t]
                if kind == 1:
                    key = r0 + lax.broadcasted_iota(jnp.int32, (rows, t), 0)
                    qry = lax.broadcasted_iota(jnp.int32, (rows, t), 1)
                    x = jnp.where(key // CHUNK <= qry // CHUNK, x, NEG_INF)
                bias_ref[kind, r0:r0 + rows, :] = x

    for src_ref, dst_ref in ((qT_ref, qpad_ref), (qT_next_ref, qpad_next_ref)):
        qT = src_ref[0]
        row = lax.broadcasted_iota(jnp.int32, qT.shape, 0)
        zero = jnp.zeros_like(qT)
        dst_ref[0] = jnp.where(row < HEAD_DIM, qT, zero)
        dst_ref[1] = jnp.where(row >= HEAD_DIM, qT, zero)

    m_ref[...] = jnp.full(m_ref.shape, NEG_INF, _F32)
    acc_ref[...] = jnp.zeros(acc_ref.shape, _F32)
    p_ref[...] = jnp.zeros(p_ref.shape, _BF16)
    no_rescale = (jnp.ones((1, t), _F32),) * 2

    def scores(tile, q_ref=qpad_ref):
        kt = k_ref[tile]
        tile_max = []
        for c in range(2):
            s = jnp.dot(kt, q_ref[c], preferred_element_type=_F32)
            s_ref[c] = s
            tile_max.append(jnp.max(s, axis=0, keepdims=True))
        return tuple(tile_max)

    def softmax(bias_kind, tile_max):
        alphas = []
        for c in range(2):
            s = s_ref[c]
            if bias_kind is None:
                s_max = tile_max[c]
            else:
                s = s + bias_ref[bias_kind]
                s_max = jnp.max(s, axis=0, keepdims=True)
            m_old = m_ref[c]
            m_new = jnp.maximum(m_old, s_max)
            alpha = jnp.exp2(m_old - m_new)
            p = jnp.exp2(s - m_new)
            p_ref[c] = p.astype(_BF16)
            m_ref[c] = m_new
            alphas.append(alpha)
        return tuple(alphas)

    def accumulate(tile, alphas):
        vt = vT_ref[jnp.maximum(tile, 0)]
        for c in range(2):
            acc_ref[c] = alphas[c] * acc_ref[c] + jnp.dot(vt, p_ref[c],
                                                          preferred_element_type=_F32)

    def scores_for_next_step():
        for c, tile_max in enumerate(scores(0, qpad_next_ref)):
            first_max_ref[c] = tile_max

    n_far = jnp.maximum(i - 1, 0)

    @pl.when(i == 0)
    def _():
        for c, tile_max in enumerate(scores(0)):
            first_max_ref[c] = tile_max

    first_max = (first_max_ref[0], first_max_ref[1])

    def far_step(tile, carry):
        pending, tile_max = carry
        accumulate(tile - 1, pending)
        pending = softmax(None, tile_max)
        return pending, scores(tile + 1)

    def far_pair(jj, carry):
        return far_step(2 * jj + 1, far_step(2 * jj, carry))

    carry = lax.fori_loop(0, n_far // 2, far_pair, (no_rescale, first_max))
    pending, _ = lax.cond(n_far % 2 == 1, lambda c: far_step(n_far - 1, c), lambda c: c, carry)

    @pl.when(i == 0)
    def _():
        alphas = softmax(1, None)
        scores_for_next_step()
        accumulate(0, alphas)

    @pl.when(i > 0)
    def _():
        accumulate(i - 2, pending)
        alphas = softmax(0, None)
        scores(i)
        accumulate(i - 1, alphas)
        alphas = softmax(1, None)
        scores_for_next_step()
        accumulate(i, alphas)

    lam = lam_ref[0, 0]
    o = (acc_ref[0, :V_HEAD_DIM] / acc_ref[0, V_HEAD_DIM:V_HEAD_DIM + 1]
         - lam * (acc_ref[1, :V_HEAD_DIM] / acc_ref[1, V_HEAD_DIM:V_HEAD_DIM + 1]))
    ms = jnp.mean(o * o, axis=0, keepdims=True)
    o = o * lax.rsqrt(ms + SUBLN_EPS) * g_ref[...] * (1.0 - LAMBDA_INIT)
    o_ref[...] = o.T.astype(_BF16)


def _diff_attn(lam, qT, k, vT, bias_vec, subln_g_col):
    n_tiles, _, t = qT.shape
    seq = n_tiles * t
    n_vec = bias_vec.shape[2]
    return pl.pallas_call(
        _diff_attn_kernel,
        grid=(N_HEADS, n_tiles),
        in_specs=[
            pl.BlockSpec(memory_space=pltpu.SMEM),
            pl.BlockSpec((1, 2 * HEAD_DIM, t), lambda h, i: (i, h, 0)),
            pl.BlockSpec((1, 2 * HEAD_DIM, t), lambda h, i: (jnp.minimum(i + 1, n_tiles - 1), h, 0)),
            pl.BlockSpec((n_tiles, t, 2 * HEAD_DIM), lambda h, i: (0, 0, h)),
            pl.BlockSpec((n_tiles, V_AUG_DIM, t), lambda h, i: (0, h, 0)),
            pl.BlockSpec((1, 1, n_vec), lambda h, i: (h, 0, 0)),
            pl.BlockSpec((V_HEAD_DIM, 1), lambda h, i: (0, 0)),
        ],
        out_specs=pl.BlockSpec((t, V_HEAD_DIM), lambda h, i: (i, h)),
        out_shape=jax.ShapeDtypeStruct((seq, ATTN_WIDTH), _BF16),
        scratch_shapes=[
            pltpu.VMEM((2, 2 * HEAD_DIM, t), _BF16),
            pltpu.VMEM((2, 2 * HEAD_DIM, t), _BF16),
            pltpu.VMEM((2, t, t), _F32),
            pltpu.VMEM((2, t, t), _F32),
            pltpu.VMEM((2, t, t), _BF16),
            pltpu.VMEM((2, 1, t), _F32),
            pltpu.VMEM((2, V_AUG_DIM, t), _F32),
            pltpu.VMEM((2, 1, t), _F32),
        ],
        compiler_params=pltpu.CompilerParams(
            dimension_semantics=("arbitrary", "arbitrary"), vmem_limit_bytes=VMEM_LIMIT_BYTES),
        name="diff_attn",
    )(lam, qT, qT, k, vT, bias_vec, subln_g_col)


def _t5_bucket(rel):
    nb = NUM_BUCKETS // 2
    max_exact = nb // 2
    bucket = (rel > 0).astype(jnp.int32) * nb
    n = jnp.abs(rel)
    n_f = jnp.maximum(n, 1).astype(jnp.float32)
    large = max_exact + (jnp.log(n_f / max_exact) / math.log(MAX_DISTANCE / max_exact)
                         * (nb - max_exact)).astype(jnp.int32)
    large = jnp.minimum(large, nb - 1)
    return bucket + jnp.where(n < max_exact, n, large)


def _near_bias_vector(rel_table, t):
    assert t >= MAX_DISTANCE and t % CHUNK == 0
    rel = jnp.arange(3 * t - 1, dtype=jnp.int32) - (2 * t - 1)
    far = rel_table[_t5_bucket(jnp.int32(-(t + 1)))].astype(_F32)
    vec = (rel_table[_t5_bucket(rel)].astype(_F32) - far) * LOG2_E
    vec = jnp.pad(vec[::-1], ((0, 1), (0, 0)))
    return vec.T.reshape(rel_table.shape[1], 1, 3 * t)


def _mixer_out_kernel(x_ref, on_ref, ga_ref, gp_ref, w_pa_ref, w_out_ref, g_mlp_ref,
                      w_up_ref, w_down_ref, g_fin_ref, out_ref):
    y_attn = jnp.dot(on_ref[...], w_pa_ref[...], preferred_element_type=_F32)
    merged = ga_ref[...].astype(_F32) * y_attn + gp_ref[...]
    x1 = x_ref[...] + jnp.dot(merged.astype(_BF16), w_out_ref[...], preferred_element_type=_F32)
    h = _rms_norm(x1, g_mlp_ref[...], NORM_EPS).astype(_BF16)
    x2 = x1
    for c in range(0, D_FF, FF_CHUNK):
        up = jnp.dot(h, w_up_ref[:, c:c + FF_CHUNK], preferred_element_type=_F32)
        act = jnp.square(jnp.maximum(up, 0.0)).astype(_BF16)
        x2 = x2 + jnp.dot(act, w_down_ref[c:c + FF_CHUNK, :], preferred_element_type=_F32)
    out_ref[...] = _rms_norm(x2, g_fin_ref[...], NORM_EPS)


def _mixer_out(x2d, on, ga, gp, w_pa, w_out, g_mlp, w_up, w_down, g_fin):
    seq = x2d.shape[0]
    tm = SEQ_TILE
    row_tile = pl.BlockSpec((tm, D_MODEL), lambda i: (i, 0))
    return pl.pallas_call(
        _mixer_out_kernel,
        grid=(seq // tm,),
        in_specs=[
            row_tile, row_tile, row_tile, row_tile,
            _resident((ATTN_WIDTH, D_MODEL)),
            _resident((D_MODEL, D_MODEL)),
            _resident((1, D_MODEL)),
            _resident((D_MODEL, D_FF)),
            _resident((D_FF, D_MODEL)),
            _resident((1, D_MODEL)),
        ],
        out_specs=row_tile,
        out_shape=jax.ShapeDtypeStruct((seq, D_MODEL), _F32),
        compiler_params=pltpu.CompilerParams(
            dimension_semantics=("arbitrary",), vmem_limit_bytes=VMEM_LIMIT_BYTES),
        name="mixer_out",
    )(x2d, on, ga, gp, w_pa, w_out, g_mlp, w_up, w_down, g_fin)


def kernel(x, ln_mix_g, w_in, rel_bias_table, lambda_q1, lambda_k1, lambda_q2, lambda_k2,
           subln_g, w_proj_attn, w_pool_grp, pool_scale, w_proj_pool, w_out,
           ln_mlp_g, w_mlp_up, w_mlp_down, ln_final_g):
    batch, seq, d_model = x.shape
    assert batch == 1 and d_model == D_MODEL and seq % SEQ_TILE == 0
    assert ln_mix_g.shape[0] == 1, "single-layer trunk"
    x2d = x.reshape(seq, d_model)

    qT, k, vT, ga, gp = _mixer_in(
        x2d, ln_mix_g, w_in[0].astype(_BF16), w_pool_grp[0].astype(_BF16), pool_scale,
        w_proj_pool[0].astype(_BF16))

    lam = pl.pallas_call(
        _lambda_kernel, out_shape=jax.ShapeDtypeStruct((1, 1), _F32), name="diff_lambda",
    )(lambda_q1, lambda_k1, lambda_q2, lambda_k2)
    bias_vec = _near_bias_vector(rel_bias_table, SEQ_TILE)
    on = _diff_attn(lam, qT, k, vT, bias_vec, subln_g.reshape(V_HEAD_DIM, 1))

    out = _mixer_out(
        x2d, on, ga, gp, w_proj_attn[0].astype(_BF16), w_out[0].astype(_BF16), ln_mlp_g,
        w_mlp_up[0].astype(_BF16), w_mlp_down[0].astype(_BF16), ln_final_g.reshape(1, d_model))
    return out.reshape(batch, seq, d_model)
```

```python
import functools
import math

import jax
import jax.numpy as jnp
from jax import lax
from jax.experimental import pallas as pl
from jax.experimental.pallas import tpu as pltpu

D_MODEL = 1024
N_HEADS = 8
HEAD_DIM = 64
V_HEAD_DIM = 2 * HEAD_DIM
QK_WIDTH = N_HEADS * 2 * HEAD_DIM
ATTN_WIDTH = N_HEADS * V_HEAD_DIM
BF16_SUBLANES = 16
V_AUG_DIM = V_HEAD_DIM + BF16_SUBLANES
POOL_WINDOWS = (2, 4, 8, 16)
POOL_GROUP_WIDTH = D_MODEL // len(POOL_WINDOWS)
MAX_WINDOW = max(POOL_WINDOWS)
D_FF = 4 * D_MODEL
CHUNK = 64
NUM_BUCKETS = 32
MAX_DISTANCE = 128
NORM_EPS = 1e-6
SUBLN_EPS = 1e-5
NEG_INF = -1e30
LAMBDA_INIT = 0.8 - 0.6 * math.exp(-0.3 * 0)
LOG2_E = math.log2(math.e)

SEQ_TILE = 512
FF_CHUNK = 1024
VMEM_LIMIT_BYTES = 56 * 1024 * 1024

_BF16 = jnp.bfloat16
_F32 = jnp.float32


def _rms_norm(x, g, eps):
    return x * lax.rsqrt(jnp.mean(x * x, axis=-1, keepdims=True) + eps) * g


def _resident(shape):
    return pl.BlockSpec(shape, lambda *_: (0,) * len(shape), pipeline_mode=pl.Buffered(1))


def _mixer_in_kernel(x_ref, g_ref, w_in_ref, w_grp_ref, pscale_ref, w_pp_ref,
                     qT_ref, k_ref, vT_ref, ga_ref, gp_ref, ubuf_ref):
    i = pl.program_id(0)
    tm = x_ref.shape[0]
    h = _rms_norm(x_ref[...], g_ref[...], NORM_EPS).astype(_BF16)

    def proj(col, width):
        return jnp.dot(h, w_in_ref[:, col:col + width], preferred_element_type=_F32)

    q = proj(0, QK_WIDTH) * (HEAD_DIM ** -0.5 * LOG2_E)
    qT_ref[0] = q.T.astype(_BF16)
    k_ref[0] = proj(QK_WIDTH, QK_WIDTH).astype(_BF16)
    vT = proj(2 * QK_WIDTH, ATTN_WIDTH).T.astype(_BF16)
    ones = jnp.ones((BF16_SUBLANES, tm), _BF16)
    for hd in range(N_HEADS):
        vT_ref[0, hd * V_AUG_DIM:hd * V_AUG_DIM + V_HEAD_DIM, :] = (
            vT[hd * V_HEAD_DIM:(hd + 1) * V_HEAD_DIM, :])
        vT_ref[0, hd * V_AUG_DIM + V_HEAD_DIM:(hd + 1) * V_AUG_DIM, :] = ones

    @pl.when(i == 0)
    def _():
        ubuf_ref[0:MAX_WINDOW, :] = jnp.zeros((MAX_WINDOW, D_MODEL), _F32)

    ubuf_ref[MAX_WINDOW:MAX_WINDOW + tm, :] = proj(2 * QK_WIDTH + ATTN_WIDTH, D_MODEL)
    pos = i * tm + lax.broadcasted_iota(jnp.int32, (tm, 1), 0)
    y_pool = jnp.zeros((tm, D_MODEL), _F32)
    for g, w in enumerate(POOL_WINDOWS):
        cols = slice(g * POOL_GROUP_WIDTH, (g + 1) * POOL_GROUP_WIDTH)
        u = ubuf_ref[MAX_WINDOW:MAX_WINDOW + tm, cols]
        window_sum = u
        for s in range(1, w):
            window_sum = window_sum + ubuf_ref[MAX_WINDOW - s:MAX_WINDOW - s + tm, cols]
        count = jnp.minimum(pos + 1, w).astype(_F32)
        z = window_sum / count - u
        zg = jnp.dot(z.astype(_BF16), w_grp_ref[g], preferred_element_type=_F32)
        zg = zg * pscale_ref[:, cols]
        y_pool = y_pool + jnp.dot(zg.astype(_BF16), w_pp_ref[cols, :],
                                  preferred_element_type=_F32)
    ubuf_ref[0:MAX_WINDOW, :] = ubuf_ref[tm:tm + MAX_WINDOW, :]

    gate_col = 2 * QK_WIDTH + ATTN_WIDTH + D_MODEL
    ga_ref[...] = jax.nn.sigmoid(proj(gate_col, D_MODEL)).astype(_BF16)
    gp_ref[...] = jax.nn.sigmoid(proj(gate_col + D_MODEL, D_MODEL)) * y_pool


def _mixer_in(x2d, ln_g, w_in, w_grp, pool_scale, w_pp):
    seq = x2d.shape[0]
    tm = SEQ_TILE
    n_tiles = seq // tm
    in_width = w_in.shape[1]
    row_tile = lambda i: (i, 0)
    tile3 = lambda i: (i, 0, 0)
    return pl.pallas_call(
        _mixer_in_kernel,
        grid=(n_tiles,),
        in_specs=[
            pl.BlockSpec((tm, D_MODEL), row_tile),
            _resident((1, D_MODEL)),
            _resident((D_MODEL, in_width)),
            _resident(w_grp.shape),
            _resident((1, D_MODEL)),
            _resident((D_MODEL, D_MODEL)),
        ],
        out_specs=[
            pl.BlockSpec((1, QK_WIDTH, tm), tile3),
            pl.BlockSpec((1, tm, QK_WIDTH), tile3),
            pl.BlockSpec((1, N_HEADS * V_AUG_DIM, tm), tile3),
            pl.BlockSpec((tm, D_MODEL), row_tile),
            pl.BlockSpec((tm, D_MODEL), row_tile),
        ],
        out_shape=[
            jax.ShapeDtypeStruct((n_tiles, QK_WIDTH, tm), _BF16),
            jax.ShapeDtypeStruct((n_tiles, tm, QK_WIDTH), _BF16),
            jax.ShapeDtypeStruct((n_tiles, N_HEADS * V_AUG_DIM, tm), _BF16),
            jax.ShapeDtypeStruct((seq, D_MODEL), _BF16),
            jax.ShapeDtypeStruct((seq, D_MODEL), _F32),
        ],
        scratch_shapes=[pltpu.VMEM((tm + MAX_WINDOW, D_MODEL), _F32)],
        compiler_params=pltpu.CompilerParams(
            dimension_semantics=("arbitrary",), vmem_limit_bytes=VMEM_LIMIT_BYTES),
        name="mixer_in",
    )(x2d, ln_g, w_in, w_grp, pool_scale, w_pp)


def _lambda_kernel(lq1_ref, lk1_ref, lq2_ref, lk2_ref, lam_ref):
    a = jnp.sum(lq1_ref[...] * lk1_ref[...], axis=-1, keepdims=True)
    b = jnp.sum(lq2_ref[...] * lk2_ref[...], axis=-1, keepdims=True)
    lam_ref[...] = jnp.exp(a) - jnp.exp(b) + LAMBDA_INIT


def _diff_attn_kernel(lam_ref, qT_ref, qT_next_ref, k_ref, vT_ref, bvec_ref, g_ref, o_ref,
                      qpad_ref, qpad_next_ref, bias_ref, s_ref, p_ref, m_ref, acc_ref,
                      first_max_ref):
    i = pl.program_id(1)
    t = qT_ref.shape[2]
    n_vec = bvec_ref.shape[2]

    @pl.when(i == 0)
    def _():
        rows = 64
        for kind, c0 in ((0, 2 * t - 1), (1, t - 1)):
            for r0 in range(0 if kind == 1 else t - MAX_DISTANCE, t, rows):
                x = jnp.broadcast_to(bvec_ref[0], (rows, n_vec))
                x = pltpu.roll(x, (n_vec - c0 + r0) % n_vec, 1, stride=1, stride_axis=0)[:, :t]
                if kind == 1:
                    key = r0 + lax.broadcasted_iota(jnp.int32, (rows, t), 0)
                    qry = lax.broadcasted_iota(jnp.int32, (rows, t), 1)
                    x = jnp.where(key // CHUNK <= qry // CHUNK, x, NEG_INF)
                bias_ref[kind, r0:r0 + rows, :] = x

    for src_ref, dst_ref in ((qT_ref, qpad_ref), (qT_next_ref, qpad_next_ref)):
        qT = src_ref[0]
        row = lax.broadcasted_iota(jnp.int32, qT.shape, 0)
        zero = jnp.zeros_like(qT)
        dst_ref[0] = jnp.where(row < HEAD_DIM, qT, zero)
        dst_ref[1] = jnp.where(row >= HEAD_DIM, qT, zero)

    m_ref[...] = jnp.full(m_ref.shape, NEG_INF, _F32)
    acc_ref[...] = jnp.zeros(acc_ref.shape, _F32)
    p_ref[...] = jnp.zeros(p_ref.shape, _BF16)
    no_rescale = (jnp.ones((1, t), _F32),) * 2

    def scores(tile, q_ref=qpad_ref):
        kt = k_ref[tile]
        tile_max = []
        for c in range(2):
            s = jnp.dot(kt, q_ref[c], preferred_element_type=_F32)
            s_ref[c] = s
            tile_max.append(jnp.max(s, axis=0, keepdims=True))
        return tuple(tile_max)

    def softmax(tile_max):
        alphas = []
        for c in range(2):
            m_old = m_ref[c]
            m_new = jnp.maximum(m_old, tile_max[c])
            p_ref[c] = jnp.exp2(s_ref[c] - m_new).astype(_BF16)
            m_ref[c] = m_new
            alphas.append(jnp.exp2(m_old - m_new))
        return tuple(alphas)

    def accumulate(tile, alphas):
        vt = vT_ref[jnp.maximum(tile, 0)]
        for c in range(2):
            acc_ref[c] = alphas[c] * acc_ref[c] + jnp.dot(vt, p_ref[c],
                                                          preferred_element_type=_F32)

    half = t // 2
    top, bot = slice(0, half), slice(half, t)

    def scores_diag(tile):
        kt = k_ref[tile]
        for c in range(2):
            s_ref[c, top, :] = jnp.dot(kt[top], qpad_ref[c], preferred_element_type=_F32)
            s_ref[c, bot, bot] = jnp.dot(kt[bot], qpad_ref[c, :, bot],
                                         preferred_element_type=_F32)

    def softmax_diag():
        alphas = []
        for c in range(2):
            s_top = s_ref[c, top, :] + bias_ref[1, top, :]
            s_bot = s_ref[c, bot, bot] + bias_ref[1, bot, bot]
            max_top = jnp.max(s_top, axis=0, keepdims=True)
            max_bot = jnp.max(s_bot, axis=0, keepdims=True)
            s_max = jnp.concatenate(
                [max_top[:, top], jnp.maximum(max_top[:, bot], max_bot)], axis=1)
            m_old = m_ref[c]
            m_new = jnp.maximum(m_old, s_max)
            p_ref[c, top, :] = jnp.exp2(s_top - m_new).astype(_BF16)
            p_ref[c, bot, bot] = jnp.exp2(s_bot - m_new[:, bot]).astype(_BF16)
            m_ref[c] = m_new
            alphas.append(jnp.exp2(m_old - m_new))
        return tuple(alphas)

    def accumulate_diag(tile, alphas):
        vt = vT_ref[tile]
        for c in range(2):
            acc_ref[c, :, top] = alphas[c][:, top] * acc_ref[c, :, top] + jnp.dot(
                vt[:, top], p_ref[c, top, top], preferred_element_type=_F32)
            acc_ref[c, :, bot] = alphas[c][:, bot] * acc_ref[c, :, bot] + jnp.dot(
                vt, p_ref[c, :, bot], preferred_element_type=_F32)

    def scores_for_next_step():
        for c, tile_max in enumerate(scores(0, qpad_next_ref)):
            first_max_ref[c] = tile_max

    n_far = jnp.maximum(i - 1, 0)

    @pl.when(i == 0)
    def _():
        for c, tile_max in enumerate(scores(0)):
            first_max_ref[c] = tile_max

    first_max = (first_max_ref[0], first_max_ref[1])

    def far_step(tile, carry):
        pending, tile_max = carry
        accumulate(tile - 1, pending)
        pending = softmax(tile_max)
        return pending, scores(tile + 1)

    def far_pair(jj, carry):
        return far_step(2 * jj + 1, far_step(2 * jj, carry))

    carry = lax.fori_loop(0, n_far // 2, far_pair, (no_rescale, first_max))
    pending, prev_max = lax.cond(n_far % 2 == 1, lambda c: far_step(n_far - 1, c), lambda c: c,
                                 carry)

    def add_corner_bias(tile_max):
        keys, qrys = slice(t - MAX_DISTANCE, t), slice(0, MAX_DISTANCE)
        lane = lax.broadcasted_iota(jnp.int32, (1, t), 1)
        bounds = []
        for c in range(2):
            blk = s_ref[c, keys, qrys] + bias_ref[0, keys, qrys]
            s_ref[c, keys, qrys] = blk
            blk_max = jnp.max(blk, axis=0, keepdims=True)
            blk_max = jnp.concatenate([blk_max] * (t // MAX_DISTANCE), axis=1)
            bounds.append(jnp.where(lane < MAX_DISTANCE, jnp.maximum(tile_max[c], blk_max),
                                    tile_max[c]))
        return tuple(bounds)

    @pl.when(i == 0)
    def _():
        alphas = softmax_diag()
        scores_for_next_step()
        accumulate_diag(0, alphas)

    @pl.when(i > 0)
    def _():
        accumulate(i - 2, pending)
        alphas = softmax(add_corner_bias(prev_max))
        scores_diag(i)
        accumulate(i - 1, alphas)
        alphas = softmax_diag()
        scores_for_next_step()
        accumulate_diag(i, alphas)

    lam = lam_ref[0, 0]
    o = (acc_ref[0, :V_HEAD_DIM] / acc_ref[0, V_HEAD_DIM:V_HEAD_DIM + 1]
         - lam * (acc_ref[1, :V_HEAD_DIM] / acc_ref[1, V_HEAD_DIM:V_HEAD_DIM + 1]))
    ms = jnp.mean(o * o, axis=0, keepdims=True)
    o = o * lax.rsqrt(ms + SUBLN_EPS) * g_ref[...] * (1.0 - LAMBDA_INIT)
    o_ref[...] = o.T.astype(_BF16)


def _diff_attn(lam, qT, k, vT, bias_vec, subln_g_col):
    n_tiles, _, t = qT.shape
    seq = n_tiles * t
    n_vec = bias_vec.shape[2]
    return pl.pallas_call(
        _diff_attn_kernel,
        grid=(N_HEADS, n_tiles),
        in_specs=[
            pl.BlockSpec(memory_space=pltpu.SMEM),
            pl.BlockSpec((1, 2 * HEAD_DIM, t), lambda h, i: (i, h, 0)),
            pl.BlockSpec((1, 2 * HEAD_DIM, t), lambda h, i: (jnp.minimum(i + 1, n_tiles - 1), h, 0)),
            pl.BlockSpec((n_tiles, t, 2 * HEAD_DIM), lambda h, i: (0, 0, h)),
            pl.BlockSpec((n_tiles, V_AUG_DIM, t), lambda h, i: (0, h, 0)),
            pl.BlockSpec((1, 1, n_vec), lambda h, i: (h, 0, 0)),
            pl.BlockSpec((V_HEAD_DIM, 1), lambda h, i: (0, 0)),
        ],
        out_specs=pl.BlockSpec((t, V_HEAD_DIM), lambda h, i: (i, h)),
        out_shape=jax.ShapeDtypeStruct((seq, ATTN_WIDTH), _BF16),
        scratch_shapes=[
            pltpu.VMEM((2, 2 * HEAD_DIM, t), _BF16),
            pltpu.VMEM((2, 2 * HEAD_DIM, t), _BF16),
            pltpu.VMEM((2, t, t), _F32),
            pltpu.VMEM((2, t, t), _F32),
            pltpu.VMEM((2, t, t), _BF16),
            pltpu.VMEM((2, 1, t), _F32),
            pltpu.VMEM((2, V_AUG_DIM, t), _F32),
            pltpu.VMEM((2, 1, t), _F32),
        ],
        compiler_params=pltpu.CompilerParams(
            dimension_semantics=("arbitrary", "arbitrary"), vmem_limit_bytes=VMEM_LIMIT_BYTES),
        name="diff_attn",
    )(lam, qT, qT, k, vT, bias_vec, subln_g_col)


def _t5_bucket(rel):
    nb = NUM_BUCKETS // 2
    max_exact = nb // 2
    bucket = (rel > 0).astype(jnp.int32) * nb
    n = jnp.abs(rel)
    n_f = jnp.maximum(n, 1).astype(jnp.float32)
    large = max_exact + (jnp.log(n_f / max_exact) / math.log(MAX_DISTANCE / max_exact)
                         * (nb - max_exact)).astype(jnp.int32)
    large = jnp.minimum(large, nb - 1)
    return bucket + jnp.where(n < max_exact, n, large)


def _near_bias_vector(rel_table, t):
    assert t >= MAX_DISTANCE and t % CHUNK == 0
    rel = jnp.arange(3 * t - 1, dtype=jnp.int32) - (2 * t - 1)
    far = rel_table[_t5_bucket(jnp.int32(-(t + 1)))].astype(_F32)
    vec = (rel_table[_t5_bucket(rel)].astype(_F32) - far) * LOG2_E
    vec = jnp.pad(vec[::-1], ((0, 1), (0, 0)))
    return vec.T.reshape(rel_table.shape[1], 1, 3 * t)


def _mixer_out_kernel(x_ref, on_ref, ga_ref, gp_ref, w_pa_ref, w_out_ref, g_mlp_ref,
                      w_up_ref, w_down_ref, g_fin_ref, out_ref):
    y_attn = jnp.dot(on_ref[...], w_pa_ref[...], preferred_element_type=_F32)
    merged = ga_ref[...].astype(_F32) * y_attn + gp_ref[...]
    x1 = x_ref[...] + jnp.dot(merged.astype(_BF16), w_out_ref[...], preferred_element_type=_F32)
    h = _rms_norm(x1, g_mlp_ref[...], NORM_EPS).astype(_BF16)
    x2 = x1
    for c in range(0, D_FF, FF_CHUNK):
        up = jnp.dot(h, w_up_ref[:, c:c + FF_CHUNK], preferred_element_type=_F32)
        act = jnp.square(jnp.maximum(up, 0.0)).astype(_BF16)
        x2 = x2 + jnp.dot(act, w_down_ref[c:c + FF_CHUNK, :], preferred_element_type=_F32)
    out_ref[...] = _rms_norm(x2, g_fin_ref[...], NORM_EPS)


def _mixer_out(x2d, on, ga, gp, w_pa, w_out, g_mlp, w_up, w_down, g_fin):
    seq = x2d.shape[0]
    tm = SEQ_TILE
    row_tile = pl.BlockSpec((tm, D_MODEL), lambda i: (i, 0))
    return pl.pallas_call(
        _mixer_out_kernel,
        grid=(seq // tm,),
        in_specs=[
            row_tile, row_tile, row_tile, row_tile,
            _resident((ATTN_WIDTH, D_MODEL)),
            _resident((D_MODEL, D_MODEL)),
            _resident((1, D_MODEL)),
            _resident((D_MODEL, D_FF)),
            _resident((D_FF, D_MODEL)),
            _resident((1, D_MODEL)),
        ],
        out_specs=row_tile,
        out_shape=jax.ShapeDtypeStruct((seq, D_MODEL), _F32),
        compiler_params=pltpu.CompilerParams(
            dimension_semantics=("arbitrary",), vmem_limit_bytes=VMEM_LIMIT_BYTES),
        name="mixer_out",
    )(x2d, on, ga, gp, w_pa, w_out, g_mlp, w_up, w_down, g_fin)


def kernel(x, ln_mix_g, w_in, rel_bias_table, lambda_q1, lambda_k1, lambda_q2, lambda_k2,
           subln_g, w_proj_attn, w_pool_grp, pool_scale, w_proj_pool, w_out,
           ln_mlp_g, w_mlp_up, w_mlp_down, ln_final_g):
    batch, seq, d_model = x.shape
    assert batch == 1 and d_model == D_MODEL and seq % SEQ_TILE == 0
    assert ln_mix_g.shape[0] == 1, "single-layer trunk"
    x2d = x.reshape(seq, d_model)

    qT, k, vT, ga, gp = _mixer_in(
        x2d, ln_mix_g, w_in[0].astype(_BF16), w_pool_grp[0].astype(_BF16), pool_scale,
        w_proj_pool[0].astype(_BF16))

    lam = pl.pallas_call(
        _lambda_kernel, out_shape=jax.ShapeDtypeStruct((1, 1), _F32), name="diff_lambda",
    )(lambda_q1, lambda_k1, lambda_q2, lambda_k2)
    bias_vec = _near_bias_vector(rel_bias_table, SEQ_TILE)
    on = _diff_attn(lam, qT, k, vT, bias_vec, subln_g.reshape(V_HEAD_DIM, 1))

    out = _mixer_out(
        x2d, on, ga, gp, w_proj_attn[0].astype(_BF16), w_out[0].astype(_BF16), ln_mlp_g,
        w_mlp_up[0].astype(_BF16), w_mlp_down[0].astype(_BF16), ln_final_g.reshape(1, d_model))
    return out.reshape(batch, seq, d_model)
```

```python
import functools
import math

import jax
import jax.numpy as jnp
from jax import lax
from jax.experimental import pallas as pl
from jax.experimental.pallas import tpu as pltpu

D_MODEL = 1024
N_HEADS = 8
HEAD_DIM = 64
V_HEAD_DIM = 2 * HEAD_DIM
QK_WIDTH = N_HEADS * 2 * HEAD_DIM
ATTN_WIDTH = N_HEADS * V_HEAD_DIM
BF16_SUBLANES = 16
V_AUG_DIM = V_HEAD_DIM + BF16_SUBLANES
POOL_WINDOWS = (2, 4, 8, 16)
POOL_GROUP_WIDTH = D_MODEL // len(POOL_WINDOWS)
MAX_WINDOW = max(POOL_WINDOWS)
D_FF = 4 * D_MODEL
CHUNK = 64
NUM_BUCKETS = 32
MAX_DISTANCE = 128
NORM_EPS = 1e-6
SUBLN_EPS = 1e-5
NEG_INF = -1e30
LAMBDA_INIT = 0.8 - 0.6 * math.exp(-0.3 * 0)
LOG2_E = math.log2(math.e)

SEQ_TILE = 512
FF_CHUNK = 1024
VMEM_LIMIT_BYTES = 56 * 1024 * 1024

_BF16 = jnp.bfloat16
_F32 = jnp.float32


def _rms_norm(x, g, eps):
    return x * lax.rsqrt(jnp.mean(x * x, axis=-1, keepdims=True) + eps) * g


def _resident(shape):
    return pl.BlockSpec(shape, lambda *_: (0,) * len(shape), pipeline_mode=pl.Buffered(1))


def _mixer_in_kernel(x_ref, g_ref, w_in_ref, w_grp_ref, pscale_ref, w_pp_ref,
                     qT_ref, k_ref, vT_ref, ga_ref, gp_ref, ubuf_ref):
    i = pl.program_id(0)
    tm = x_ref.shape[0]
    h = _rms_norm(x_ref[...], g_ref[...], NORM_EPS).astype(_BF16)

    def proj(col, width):
        return jnp.dot(h, w_in_ref[:, col:col + width], preferred_element_type=_F32)

    q = proj(0, QK_WIDTH) * (HEAD_DIM ** -0.5 * LOG2_E)
    qT_ref[0] = q.T.astype(_BF16)
    k_ref[0] = proj(QK_WIDTH, QK_WIDTH).astype(_BF16)
    vT = proj(2 * QK_WIDTH, ATTN_WIDTH).T.astype(_BF16)
    ones = jnp.ones((BF16_SUBLANES, tm), _BF16)
    for hd in range(N_HEADS):
        vT_ref[0, hd * V_AUG_DIM:hd * V_AUG_DIM + V_HEAD_DIM, :] = (
            vT[hd * V_HEAD_DIM:(hd + 1) * V_HEAD_DIM, :])
        vT_ref[0, hd * V_AUG_DIM + V_HEAD_DIM:(hd + 1) * V_AUG_DIM, :] = ones

    @pl.when(i == 0)
    def _():
        ubuf_ref[0:MAX_WINDOW, :] = jnp.zeros((MAX_WINDOW, D_MODEL), _F32)

    ubuf_ref[MAX_WINDOW:MAX_WINDOW + tm, :] = proj(2 * QK_WIDTH + ATTN_WIDTH, D_MODEL)
    pos = i * tm + lax.broadcasted_iota(jnp.int32, (tm, 1), 0)
    y_pool = jnp.zeros((tm, D_MODEL), _F32)
    for g, w in enumerate(POOL_WINDOWS):
        cols = slice(g * POOL_GROUP_WIDTH, (g + 1) * POOL_GROUP_WIDTH)
        u = ubuf_ref[MAX_WINDOW:MAX_WINDOW + tm, cols]
        window_sum = u
        for s in range(1, w):
            window_sum = window_sum + ubuf_ref[MAX_WINDOW - s:MAX_WINDOW - s + tm, cols]
        count = jnp.minimum(pos + 1, w).astype(_F32)
        z = window_sum / count - u
        zg = jnp.dot(z.astype(_BF16), w_grp_ref[g], preferred_element_type=_F32)
        zg = zg * pscale_ref[:, cols]
        y_pool = y_pool + jnp.dot(zg.astype(_BF16), w_pp_ref[cols, :],
                                  preferred_element_type=_F32)
    ubuf_ref[0:MAX_WINDOW, :] = ubuf_ref[tm:tm + MAX_WINDOW, :]

    gate_col = 2 * QK_WIDTH + ATTN_WIDTH + D_MODEL
    ga_ref[...] = jax.nn.sigmoid(proj(gate_col, D_MODEL)).astype(_BF16)
    gp_ref[...] = jax.nn.sigmoid(proj(gate_col + D_MODEL, D_MODEL)) * y_pool


def _mixer_in(x2d, ln_g, w_in, w_grp, pool_scale, w_pp):
    seq = x2d.shape[0]
    tm = SEQ_TILE
    n_tiles = seq // tm
    in_width = w_in.shape[1]
    row_tile = lambda i: (i, 0)
    tile3 = lambda i: (i, 0, 0)
    return pl.pallas_call(
        _mixer_in_kernel,
        grid=(n_tiles,),
        in_specs=[
            pl.BlockSpec((tm, D_MODEL), row_tile),
            _resident((1, D_MODEL)),
            _resident((D_MODEL, in_width)),
            _resident(w_grp.shape),
            _resident((1, D_MODEL)),
            _resident((D_MODEL, D_MODEL)),
        ],
        out_specs=[
            pl.BlockSpec((1, QK_WIDTH, tm), tile3),
            pl.BlockSpec((1, tm, QK_WIDTH), tile3),
            pl.BlockSpec((1, N_HEADS * V_AUG_DIM, tm), tile3),
            pl.BlockSpec((tm, D_MODEL), row_tile),
            pl.BlockSpec((tm, D_MODEL), row_tile),
        ],
        out_shape=[
            jax.ShapeDtypeStruct((n_tiles, QK_WIDTH, tm), _BF16),
            jax.ShapeDtypeStruct((n_tiles, tm, QK_WIDTH), _BF16),
            jax.ShapeDtypeStruct((n_tiles, N_HEADS * V_AUG_DIM, tm), _BF16),
            jax.ShapeDtypeStruct((seq, D_MODEL), _BF16),
            jax.ShapeDtypeStruct((seq, D_MODEL), _F32),
        ],
        scratch_shapes=[pltpu.VMEM((tm + MAX_WINDOW, D_MODEL), _F32)],
        compiler_params=pltpu.CompilerParams(
            dimension_semantics=("arbitrary",), vmem_limit_bytes=VMEM_LIMIT_BYTES),
        name="mixer_in",
    )(x2d, ln_g, w_in, w_grp, pool_scale, w_pp)


def _lambda_kernel(lq1_ref, lk1_ref, lq2_ref, lk2_ref, lam_ref):
    a = jnp.sum(lq1_ref[...] * lk1_ref[...], axis=-1, keepdims=True)
    b = jnp.sum(lq2_ref[...] * lk2_ref[...], axis=-1, keepdims=True)
    lam_ref[...] = jnp.exp(a) - jnp.exp(b) + LAMBDA_INIT


def _diff_attn_kernel(lam_ref, qT_ref, qT_next_ref, k_ref, vT_ref, bvec_ref, g_ref, o_ref,
                      qpad_ref, qpad_next_ref, bias_ref, s_ref, p_ref, m_ref, acc_ref,
                      first_max_ref):
    i = pl.program_id(1)
    t = qT_ref.shape[2]
    n_vec = bvec_ref.shape[2]

    @pl.when(i == 0)
    def _():
        rows = 64
        for kind, c0 in ((0, 2 * t - 1), (1, t - 1)):
            for r0 in range(0 if kind == 1 else t - MAX_DISTANCE, t, rows):
                x = jnp.broadcast_to(bvec_ref[0], (rows, n_vec))
                x = pltpu.roll(x, (n_vec - c0 + r0) % n_vec, 1, stride=1, stride_axis=0)[:, :t]
                if kind == 1:
                    key = r0 + lax.broadcasted_iota(jnp.int32, (rows, t), 0)
                    qry = lax.broadcasted_iota(jnp.int32, (rows, t), 1)
                    x = jnp.where(key // CHUNK <= qry // CHUNK, x, NEG_INF)
                bias_ref[kind, r0:r0 + rows, :] = x

    for src_ref, dst_ref in ((qT_ref, qpad_ref), (qT_next_ref, qpad_next_ref)):
        qT = src_ref[0]
        row = lax.broadcasted_iota(jnp.int32, qT.shape, 0)
        zero = jnp.zeros_like(qT)
        dst_ref[0] = jnp.where(row < HEAD_DIM, qT, zero)
        dst_ref[1] = jnp.where(row >= HEAD_DIM, qT, zero)

    m_ref[...] = jnp.full(m_ref.shape, NEG_INF, _F32)
    acc_ref[...] = jnp.zeros(acc_ref.shape, _F32)
    p_ref[...] = jnp.zeros(p_ref.shape, _BF16)
    no_rescale = (jnp.ones((1, t), _F32),) * 2

    def scores(tile, q_ref=qpad_ref):
        kt = k_ref[tile]
        tile_max = []
        for c in range(2):
            s = jnp.dot(kt, q_ref[c], preferred_element_type=_F32)
            s_ref[c] = s
            tile_max.append(jnp.max(s, axis=0, keepdims=True))
        return tuple(tile_max)

    def softmax(tile_max):
        alphas = []
        for c in range(2):
            m_old = m_ref[c]
            m_new = jnp.maximum(m_old, tile_max[c])
            p_ref[c] = jnp.exp2(s_ref[c] - m_new).astype(_BF16)
            m_ref[c] = m_new
            alphas.append(jnp.exp2(m_old - m_new))
        return tuple(alphas)

    def accumulate(tile, alphas):
        vt = vT_ref[jnp.maximum(tile, 0)]
        for c in range(2):
            acc_ref[c] = alphas[c] * acc_ref[c] + jnp.dot(vt, p_ref[c],
                                                          preferred_element_type=_F32)

    half = t // 2
    top, bot = slice(0, half), slice(half, t)

    def scores_diag(tile):
        kt = k_ref[tile]
        for c in range(2):
            s_ref[c, top, :] = jnp.dot(kt[top], qpad_ref[c], preferred_element_type=_F32)
            s_ref[c, bot, bot] = jnp.dot(kt[bot], qpad_ref[c, :, bot],
                                         preferred_element_type=_F32)

    def softmax_diag():
        alphas = []
        for c in range(2):
            s_top = s_ref[c, top, :] + bias_ref[1, top, :]
            s_bot = s_ref[c, bot, bot] + bias_ref[1, bot, bot]
            max_top = jnp.max(s_top, axis=0, keepdims=True)
            max_bot = jnp.max(s_bot, axis=0, keepdims=True)
            s_max = jnp.concatenate(
                [max_top[:, top], jnp.maximum(max_top[:, bot], max_bot)], axis=1)
            m_old = m_ref[c]
            m_new = jnp.maximum(m_old, s_max)
            p_ref[c, top, :] = jnp.exp2(s_top - m_new).astype(_BF16)
            p_ref[c, bot, bot] = jnp.exp2(s_bot - m_new[:, bot]).astype(_BF16)
            m_ref[c] = m_new
            alphas.append(jnp.exp2(m_old - m_new))
        return tuple(alphas)

    def accumulate_diag(tile, alphas):
        vt = vT_ref[tile]
        for c in range(2):
            acc_ref[c, :, top] = alphas[c][:, top] * acc_ref[c, :, top] + jnp.dot(
                vt[:, top], p_ref[c, top, top], preferred_element_type=_F32)
            acc_ref[c, :, bot] = alphas[c][:, bot] * acc_ref[c, :, bot] + jnp.dot(
                vt, p_ref[c, :, bot], preferred_element_type=_F32)

    def scores_for_next_step():
        for c, tile_max in enumerate(scores(0, qpad_next_ref)):
            first_max_ref[c] = tile_max

    n_far = jnp.maximum(i - 1, 0)

    @pl.when(i == 0)
    def _():
        for c, tile_max in enumerate(scores(0)):
            first_max_ref[c] = tile_max

    first_max = (first_max_ref[0], first_max_ref[1])

    def far_step(tile, carry):
        pending, tile_max = carry
        accumulate(tile - 1, pending)
        pending = softmax(tile_max)
        return pending, scores(tile + 1)

    def far_pair(jj, carry):
        return far_step(2 * jj + 1, far_step(2 * jj, carry))

    carry = lax.fori_loop(0, n_far // 2, far_pair, (no_rescale, first_max))
    pending, prev_max = lax.cond(n_far % 2 == 1, lambda c: far_step(n_far - 1, c), lambda c: c,
                                 carry)

    def add_corner_bias(tile_max):
        keys, qrys = slice(t - MAX_DISTANCE, t), slice(0, MAX_DISTANCE)
        lane = lax.broadcasted_iota(jnp.int32, (1, t), 1)
        biased_max = []
        for c in range(2):
            blk = s_ref[c, keys, qrys] + bias_ref[0, keys, qrys]
            s_ref[c, keys, qrys] = blk
            col_max = jnp.maximum(jnp.max(blk, axis=0, keepdims=True),
                                  jnp.max(s_ref[c, 0:t - MAX_DISTANCE, qrys], axis=0,
                                          keepdims=True))
            col_max = jnp.concatenate([col_max] * (t // MAX_DISTANCE), axis=1)
            biased_max.append(jnp.where(lane < MAX_DISTANCE, col_max, tile_max[c]))
        return tuple(biased_max)

    @pl.when(i == 0)
    def _():
        alphas = softmax_diag()
        scores_for_next_step()
        accumulate_diag(0, alphas)

    @pl.when(i > 0)
    def _():
        accumulate(i - 2, pending)
        alphas = softmax(add_corner_bias(prev_max))
        scores_diag(i)
        accumulate(i - 1, alphas)
        alphas = softmax_diag()
        scores_for_next_step()
        accumulate_diag(i, alphas)

    lam = lam_ref[0, 0]
    o = (acc_ref[0, :V_HEAD_DIM] / acc_ref[0, V_HEAD_DIM:V_HEAD_DIM + 1]
         - lam * (acc_ref[1, :V_HEAD_DIM] / acc_ref[1, V_HEAD_DIM:V_HEAD_DIM + 1]))
    ms = jnp.mean(o * o, axis=0, keepdims=True)
    o = o * lax.rsqrt(ms + SUBLN_EPS) * g_ref[...] * (1.0 - LAMBDA_INIT)
    o_ref[...] = o.T.astype(_BF16)


def _diff_attn(lam, qT, k, vT, bias_vec, subln_g_col):
    n_tiles, _, t = qT.shape
    seq = n_tiles * t
    n_vec = bias_vec.shape[2]
    return pl.pallas_call(
        _diff_attn_kernel,
        grid=(N_HEADS, n_tiles),
        in_specs=[
            pl.BlockSpec(memory_space=pltpu.SMEM),
            pl.BlockSpec((1, 2 * HEAD_DIM, t), lambda h, i: (i, h, 0)),
            pl.BlockSpec((1, 2 * HEAD_DIM, t), lambda h, i: (jnp.minimum(i + 1, n_tiles - 1), h, 0)),
            pl.BlockSpec((n_tiles, t, 2 * HEAD_DIM), lambda h, i: (0, 0, h)),
            pl.BlockSpec((n_tiles, V_AUG_DIM, t), lambda h, i: (0, h, 0)),
            pl.BlockSpec((1, 1, n_vec), lambda h, i: (h, 0, 0)),
            pl.BlockSpec((V_HEAD_DIM, 1), lambda h, i: (0, 0)),
        ],
        out_specs=pl.BlockSpec((t, V_HEAD_DIM), lambda h, i: (i, h)),
        out_shape=jax.ShapeDtypeStruct((seq, ATTN_WIDTH), _BF16),
        scratch_shapes=[
            pltpu.VMEM((2, 2 * HEAD_DIM, t), _BF16),
            pltpu.VMEM((2, 2 * HEAD_DIM, t), _BF16),
            pltpu.VMEM((2, t, t), _F32),
            pltpu.VMEM((2, t, t), _F32),
            pltpu.VMEM((2, t, t), _BF16),
            pltpu.VMEM((2, 1, t), _F32),
            pltpu.VMEM((2, V_AUG_DIM, t), _F32),
            pltpu.VMEM((2, 1, t), _F32),
        ],
        compiler_params=pltpu.CompilerParams(
            dimension_semantics=("arbitrary", "arbitrary"), vmem_limit_bytes=VMEM_LIMIT_BYTES),
        name="diff_attn",
    )(lam, qT, qT, k, vT, bias_vec, subln_g_col)


def _t5_bucket(rel):
    nb = NUM_BUCKETS // 2
    max_exact = nb // 2
    bucket = (rel > 0).astype(jnp.int32) * nb
    n = jnp.abs(rel)
    n_f = jnp.maximum(n, 1).astype(jnp.float32)
    large = max_exact + (jnp.log(n_f / max_exact) / math.log(MAX_DISTANCE / max_exact)
                         * (nb - max_exact)).astype(jnp.int32)
    large = jnp.minimum(large, nb - 1)
    return bucket + jnp.where(n < max_exact, n, large)


def _near_bias_vector(rel_table, t):
    assert t >= MAX_DISTANCE and t % CHUNK == 0
    rel = jnp.arange(3 * t - 1, dtype=jnp.int32) - (2 * t - 1)
    far = rel_table[_t5_bucket(jnp.int32(-(t + 1)))].astype(_F32)
    vec = (rel_table[_t5_bucket(rel)].astype(_F32) - far) * LOG2_E
    vec = jnp.pad(vec[::-1], ((0, 1), (0, 0)))
    return vec.T.reshape(rel_table.shape[1], 1, 3 * t)


def _mixer_out_kernel(x_ref, on_ref, ga_ref, gp_ref, w_pa_ref, w_out_ref, g_mlp_ref,
                      w_up_ref, w_down_ref, g_fin_ref, out_ref):
    y_attn = jnp.dot(on_ref[...], w_pa_ref[...], preferred_element_type=_F32)
    merged = ga_ref[...].astype(_F32) * y_attn + gp_ref[...]
    x1 = x_ref[...] + jnp.dot(merged.astype(_BF16), w_out_ref[...], preferred_element_type=_F32)
    h = _rms_norm(x1, g_mlp_ref[...], NORM_EPS).astype(_BF16)
    x2 = x1
    for c in range(0, D_FF, FF_CHUNK):
        up = jnp.dot(h, w_up_ref[:, c:c + FF_CHUNK], preferred_element_type=_F32)
        act = jnp.square(jnp.maximum(up, 0.0)).astype(_BF16)
        x2 = x2 + jnp.dot(act, w_down_ref[c:c + FF_CHUNK, :], preferred_element_type=_F32)
    out_ref[...] = _rms_norm(x2, g_fin_ref[...], NORM_EPS)


def _mixer_out(x2d, on, ga, gp, w_pa, w_out, g_mlp, w_up, w_down, g_fin):
    seq = x2d.shape[0]
    tm = SEQ_TILE
    row_tile = pl.BlockSpec((tm, D_MODEL), lambda i: (i, 0))
    return pl.pallas_call(
        _mixer_out_kernel,
        grid=(seq // tm,),
        in_specs=[
            row_tile, row_tile, row_tile, row_tile,
            _resident((ATTN_WIDTH, D_MODEL)),
            _resident((D_MODEL, D_MODEL)),
            _resident((1, D_MODEL)),
            _resident((D_MODEL, D_FF)),
            _resident((D_FF, D_MODEL)),
            _resident((1, D_MODEL)),
        ],
        out_specs=row_tile,
        out_shape=jax.ShapeDtypeStruct((seq, D_MODEL), _F32),
        compiler_params=pltpu.CompilerParams(
            dimension_semantics=("arbitrary",), vmem_limit_bytes=VMEM_LIMIT_BYTES),
        name="mixer_out",
    )(x2d, on, ga, gp, w_pa, w_out, g_mlp, w_up, w_down, g_fin)


def kernel(x, ln_mix_g, w_in, rel_bias_table, lambda_q1, lambda_k1, lambda_q2, lambda_k2,
           subln_g, w_proj_attn, w_pool_grp, pool_scale, w_proj_pool, w_out,
           ln_mlp_g, w_mlp_up, w_mlp_down, ln_final_g):
    batch, seq, d_model = x.shape
    assert batch == 1 and d_model == D_MODEL and seq % SEQ_TILE == 0
    assert ln_mix_g.shape[0] == 1, "single-layer trunk"
    x2d = x.reshape(seq, d_model)

    qT, k, vT, ga, gp = _mixer_in(
        x2d, ln_mix_g, w_in[0].astype(_BF16), w_pool_grp[0].astype(_BF16), pool_scale,
        w_proj_pool[0].astype(_BF16))

    lam = pl.pallas_call(
        _lambda_kernel, out_shape=jax.ShapeDtypeStruct((1, 1), _F32), name="diff_lambda",
    )(lambda_q1, lambda_k1, lambda_q2, lambda_k2)
    bias_vec = _near_bias_vector(rel_bias_table, SEQ_TILE)
    on = _diff_attn(lam, qT, k, vT, bias_vec, subln_g.reshape(V_HEAD_DIM, 1))

    out = _mixer_out(
        x2d, on, ga, gp, w_proj_attn[0].astype(_BF16), w_out[0].astype(_BF16), ln_mlp_g,
        w_mlp_up[0].astype(_BF16), w_mlp_down[0].astype(_BF16), ln_final_g.reshape(1, d_model))
    return out.reshape(batch, seq, d_model)
```

```python
import math

import jax
import jax.numpy as jnp
from jax import lax
from jax.experimental import pallas as pl
from jax.experimental.pallas import tpu as pltpu

D_MODEL = 1024
N_HEADS = 8
HEAD_DIM = 64
V_HEAD_DIM = 2 * HEAD_DIM
QK_WIDTH = N_HEADS * 2 * HEAD_DIM
ATTN_WIDTH = N_HEADS * V_HEAD_DIM
BF16_SUBLANES = 16
V_AUG_DIM = V_HEAD_DIM + BF16_SUBLANES
POOL_WINDOWS = (2, 4, 8, 16)
POOL_GROUP_WIDTH = D_MODEL // len(POOL_WINDOWS)
MAX_WINDOW = max(POOL_WINDOWS)
D_FF = 4 * D_MODEL
CHUNK = 64
NUM_BUCKETS = 32
MAX_DISTANCE = 128
NORM_EPS = 1e-6
SUBLN_EPS = 1e-5
NEG_INF = -1e30
LAMBDA_INIT = 0.8 - 0.6 * math.exp(-0.3 * 0)
LOG2_E = math.log2(math.e)

SEQ_TILE = 512
FF_CHUNK = 1024
VMEM_LIMIT_BYTES = 56 * 1024 * 1024

_BF16 = jnp.bfloat16
_F32 = jnp.float32


def _rms_norm(x, g, eps):
    return x * lax.rsqrt(jnp.mean(x * x, axis=-1, keepdims=True) + eps) * g


def _resident(shape):
    return pl.BlockSpec(shape, lambda *_: (0,) * len(shape), pipeline_mode=pl.Buffered(1))


def _mixer_in_kernel(x_ref, g_ref, w_in_ref, w_grp_ref, pscale_ref, w_pp_ref,
                     qT_ref, k_ref, vT_ref, ga_ref, gp_ref, ubuf_ref):
    i = pl.program_id(0)
    tm = x_ref.shape[0]
    h = _rms_norm(x_ref[...], g_ref[...], NORM_EPS).astype(_BF16)

    def proj(col, width):
        return jnp.dot(h, w_in_ref[:, col:col + width], preferred_element_type=_F32)

    q = proj(0, QK_WIDTH) * (HEAD_DIM ** -0.5 * LOG2_E)
    qT_ref[0] = q.T.astype(_BF16)
    k_ref[0] = proj(QK_WIDTH, QK_WIDTH).astype(_BF16)
    vT = proj(2 * QK_WIDTH, ATTN_WIDTH).T.astype(_BF16)
    ones = jnp.ones((BF16_SUBLANES, tm), _BF16)
    for hd in range(N_HEADS):
        vT_ref[0, hd * V_AUG_DIM:hd * V_AUG_DIM + V_HEAD_DIM, :] = (
            vT[hd * V_HEAD_DIM:(hd + 1) * V_HEAD_DIM, :])
        vT_ref[0, hd * V_AUG_DIM + V_HEAD_DIM:(hd + 1) * V_AUG_DIM, :] = ones

    @pl.when(i == 0)
    def _():
        ubuf_ref[0:MAX_WINDOW, :] = jnp.zeros((MAX_WINDOW, D_MODEL), _F32)

    ubuf_ref[MAX_WINDOW:MAX_WINDOW + tm, :] = proj(2 * QK_WIDTH + ATTN_WIDTH, D_MODEL)
    pos = i * tm + lax.broadcasted_iota(jnp.int32, (tm, 1), 0)
    y_pool = jnp.zeros((tm, D_MODEL), _F32)
    for g, w in enumerate(POOL_WINDOWS):
        cols = slice(g * POOL_GROUP_WIDTH, (g + 1) * POOL_GROUP_WIDTH)
        u = ubuf_ref[MAX_WINDOW:MAX_WINDOW + tm, cols]
        window_sum = u
        for s in range(1, w):
            window_sum = window_sum + ubuf_ref[MAX_WINDOW - s:MAX_WINDOW - s + tm, cols]
        count = jnp.minimum(pos + 1, w).astype(_F32)
        z = window_sum / count - u
        zg = jnp.dot(z.astype(_BF16), w_grp_ref[g], preferred_element_type=_F32)
        zg = zg * pscale_ref[:, cols]
        y_pool = y_pool + jnp.dot(zg.astype(_BF16), w_pp_ref[cols, :],
                                  preferred_element_type=_F32)
    ubuf_ref[0:MAX_WINDOW, :] = ubuf_ref[tm:tm + MAX_WINDOW, :]

    gate_col = 2 * QK_WIDTH + ATTN_WIDTH + D_MODEL
    ga_ref[...] = jax.nn.sigmoid(proj(gate_col, D_MODEL)).astype(_BF16)
    gp_ref[...] = jax.nn.sigmoid(proj(gate_col + D_MODEL, D_MODEL)) * y_pool


def _mixer_in(x2d, ln_g, w_in, w_grp, pool_scale, w_pp):
    seq = x2d.shape[0]
    tm = SEQ_TILE
    n_tiles = seq // tm
    in_width = w_in.shape[1]
    row_tile = lambda i: (i, 0)
    tile3 = lambda i: (i, 0, 0)
    return pl.pallas_call(
        _mixer_in_kernel,
        grid=(n_tiles,),
        in_specs=[
            pl.BlockSpec((tm, D_MODEL), row_tile),
            _resident((1, D_MODEL)),
            _resident((D_MODEL, in_width)),
            _resident(w_grp.shape),
            _resident((1, D_MODEL)),
            _resident((D_MODEL, D_MODEL)),
        ],
        out_specs=[
            pl.BlockSpec((1, QK_WIDTH, tm), tile3),
            pl.BlockSpec((1, tm, QK_WIDTH), tile3),
            pl.BlockSpec((1, N_HEADS * V_AUG_DIM, tm), tile3),
            pl.BlockSpec((tm, D_MODEL), row_tile),
            pl.BlockSpec((tm, D_MODEL), row_tile),
        ],
        out_shape=[
            jax.ShapeDtypeStruct((n_tiles, QK_WIDTH, tm), _BF16),
            jax.ShapeDtypeStruct((n_tiles, tm, QK_WIDTH), _BF16),
            jax.ShapeDtypeStruct((n_tiles, N_HEADS * V_AUG_DIM, tm), _BF16),
            jax.ShapeDtypeStruct((seq, D_MODEL), _BF16),
            jax.ShapeDtypeStruct((seq, D_MODEL), _F32),
        ],
        scratch_shapes=[pltpu.VMEM((tm + MAX_WINDOW, D_MODEL), _F32)],
        compiler_params=pltpu.CompilerParams(
            dimension_semantics=("arbitrary",), vmem_limit_bytes=VMEM_LIMIT_BYTES),
        name="mixer_in",
    )(x2d, ln_g, w_in, w_grp, pool_scale, w_pp)


def _lambda_kernel(lq1_ref, lk1_ref, lq2_ref, lk2_ref, lam_ref):
    a = jnp.sum(lq1_ref[...] * lk1_ref[...], axis=-1, keepdims=True)
    b = jnp.sum(lq2_ref[...] * lk2_ref[...], axis=-1, keepdims=True)
    lam_ref[...] = jnp.exp(a) - jnp.exp(b) + LAMBDA_INIT


def _diff_attn_kernel(lam_ref, qT_ref, qT_next_ref, k_ref, vT_ref, bvec_ref, g_ref, o_ref,
                      qpad_ref, qpad_next_ref, bias_ref, s_ref, p_ref, m_ref, acc_ref,
                      first_max_ref):
    i = pl.program_id(1)
    t = qT_ref.shape[2]
    n_vec = bvec_ref.shape[2]

    @pl.when(i == 0)
    def _():
        rows = CHUNK
        for kind, c0 in ((0, 2 * t - 1), (1, t - 1)):
            for r0 in range(0 if kind == 1 else t - MAX_DISTANCE, t, rows):
                x = jnp.broadcast_to(bvec_ref[0], (rows, n_vec))
                x = pltpu.roll(x, (n_vec - c0 + r0) % n_vec, 1, stride=1, stride_axis=0)[:, :t]
                if kind == 1:
                    key = r0 + lax.broadcasted_iota(jnp.int32, (rows, t), 0)
                    qry = lax.broadcasted_iota(jnp.int32, (rows, t), 1)
                    x = jnp.where(key // CHUNK <= qry // CHUNK, x, NEG_INF)
                bias_ref[kind, r0:r0 + rows, :] = x

    for src_ref, dst_ref in ((qT_ref, qpad_ref), (qT_next_ref, qpad_next_ref)):
        qT = src_ref[0]
        row = lax.broadcasted_iota(jnp.int32, qT.shape, 0)
        zero = jnp.zeros_like(qT)
        dst_ref[0] = jnp.where(row < HEAD_DIM, qT, zero)
        dst_ref[1] = jnp.where(row >= HEAD_DIM, qT, zero)

    m_ref[...] = jnp.full(m_ref.shape, NEG_INF, _F32)
    acc_ref[...] = jnp.zeros(acc_ref.shape, _F32)
    p_ref[...] = jnp.zeros(p_ref.shape, _BF16)
    no_rescale = (jnp.ones((1, t), _F32),) * 2

    def scores(tile, q_ref=qpad_ref):
        kt = k_ref[tile]
        tile_max = []
        for c in range(2):
            s = jnp.dot(kt, q_ref[c], preferred_element_type=_F32)
            s_ref[c] = s
            tile_max.append(jnp.max(s, axis=0, keepdims=True))
        return tuple(tile_max)

    def softmax(tile_max):
        alphas = []
        for c in range(2):
            m_old = m_ref[c]
            m_new = jnp.maximum(m_old, tile_max[c])
            p_ref[c] = jnp.exp2(s_ref[c] - m_new).astype(_BF16)
            m_ref[c] = m_new
            alphas.append(jnp.exp2(m_old - m_new))
        return tuple(alphas)

    def accumulate(tile, alphas):
        vt = vT_ref[jnp.maximum(tile, 0)]
        for c in range(2):
            acc_ref[c] = alphas[c] * acc_ref[c] + jnp.dot(vt, p_ref[c],
                                                          preferred_element_type=_F32)

    half = t // 2
    top, bot = slice(0, half), slice(half, t)

    def scores_diag(tile):
        kt = k_ref[tile]
        for c in range(2):
            s_ref[c, top, :] = jnp.dot(kt[top], qpad_ref[c], preferred_element_type=_F32)
            s_ref[c, bot, bot] = jnp.dot(kt[bot], qpad_ref[c, :, bot],
                                         preferred_element_type=_F32)

    def softmax_diag():
        alphas = []
        for c in range(2):
            s_top = s_ref[c, top, :] + bias_ref[1, top, :]
            s_bot = s_ref[c, bot, bot] + bias_ref[1, bot, bot]
            max_top = jnp.max(s_top, axis=0, keepdims=True)
            max_bot = jnp.max(s_bot, axis=0, keepdims=True)
            s_max = jnp.concatenate(
                [max_top[:, top], jnp.maximum(max_top[:, bot], max_bot)], axis=1)
            m_old = m_ref[c]
            m_new = jnp.maximum(m_old, s_max)
            p_ref[c, top, :] = jnp.exp2(s_top - m_new).astype(_BF16)
            p_ref[c, bot, bot] = jnp.exp2(s_bot - m_new[:, bot]).astype(_BF16)
            m_ref[c] = m_new
            alphas.append(jnp.exp2(m_old - m_new))
        return tuple(alphas)

    def accumulate_diag(tile, alphas):
        vt = vT_ref[tile]
        for c in range(2):
            acc_ref[c, :, top] = alphas[c][:, top] * acc_ref[c, :, top] + jnp.dot(
                vt[:, top], p_ref[c, top, top], preferred_element_type=_F32)
            acc_ref[c, :, bot] = alphas[c][:, bot] * acc_ref[c, :, bot] + jnp.dot(
                vt, p_ref[c, :, bot], preferred_element_type=_F32)

    def scores_for_next_step():
        for c, tile_max in enumerate(scores(0, qpad_next_ref)):
            first_max_ref[c] = tile_max

    n_far = jnp.maximum(i - 1, 0)

    @pl.when(i == 0)
    def _():
        for c, tile_max in enumerate(scores(0)):
            first_max_ref[c] = tile_max

    first_max = (first_max_ref[0], first_max_ref[1])

    def far_step(tile, carry):
        pending, tile_max = carry
        accumulate(tile - 1, pending)
        pending = softmax(tile_max)
        return pending, scores(tile + 1)

    def far_pair(jj, carry):
        return far_step(2 * jj + 1, far_step(2 * jj, carry))

    carry = lax.fori_loop(0, n_far // 2, far_pair, (no_rescale, first_max))
    pending, prev_max = lax.cond(n_far % 2 == 1, lambda c: far_step(n_far - 1, c), lambda c: c,
                                 carry)

    def add_corner_bias(tile_max):
        keys, qrys = slice(t - MAX_DISTANCE, t), slice(0, MAX_DISTANCE)
        lane = lax.broadcasted_iota(jnp.int32, (1, t), 1)
        biased_max = []
        for c in range(2):
            blk = s_ref[c, keys, qrys] + bias_ref[0, keys, qrys]
            s_ref[c, keys, qrys] = blk
            col_max = jnp.maximum(jnp.max(blk, axis=0, keepdims=True),
                                  jnp.max(s_ref[c, 0:t - MAX_DISTANCE, qrys], axis=0,
                                          keepdims=True))
            col_max = jnp.concatenate([col_max] * (t // MAX_DISTANCE), axis=1)
            biased_max.append(jnp.where(lane < MAX_DISTANCE, col_max, tile_max[c]))
        return tuple(biased_max)

    @pl.when(i == 0)
    def _():
        alphas = softmax_diag()
        scores_for_next_step()
        accumulate_diag(0, alphas)

    @pl.when(i > 0)
    def _():
        accumulate(i - 2, pending)
        alphas = softmax(add_corner_bias(prev_max))
        scores_diag(i)
        accumulate(i - 1, alphas)
        alphas = softmax_diag()
        scores_for_next_step()
        accumulate_diag(i, alphas)

    lam = lam_ref[0, 0]
    o = (acc_ref[0, :V_HEAD_DIM] / acc_ref[0, V_HEAD_DIM:V_HEAD_DIM + 1]
         - lam * (acc_ref[1, :V_HEAD_DIM] / acc_ref[1, V_HEAD_DIM:V_HEAD_DIM + 1]))
    ms = jnp.mean(o * o, axis=0, keepdims=True)
    o = o * lax.rsqrt(ms + SUBLN_EPS) * g_ref[...] * (1.0 - LAMBDA_INIT)
    o_ref[...] = o.T.astype(_BF16)


def _diff_attn(lam, qT, k, vT, bias_vec, subln_g_col):
    n_tiles, _, t = qT.shape
    seq = n_tiles * t
    n_vec = bias_vec.shape[2]
    return pl.pallas_call(
        _diff_attn_kernel,
        grid=(N_HEADS, n_tiles),
        in_specs=[
            pl.BlockSpec(memory_space=pltpu.SMEM),
            pl.BlockSpec((1, 2 * HEAD_DIM, t), lambda h, i: (i, h, 0)),
            pl.BlockSpec((1, 2 * HEAD_DIM, t),
                         lambda h, i: (jnp.minimum(i + 1, n_tiles - 1), h, 0)),
            pl.BlockSpec((n_tiles, t, 2 * HEAD_DIM), lambda h, i: (0, 0, h)),
            pl.BlockSpec((n_tiles, V_AUG_DIM, t), lambda h, i: (0, h, 0)),
            pl.BlockSpec((1, 1, n_vec), lambda h, i: (h, 0, 0)),
            pl.BlockSpec((V_HEAD_DIM, 1), lambda h, i: (0, 0)),
        ],
        out_specs=pl.BlockSpec((t, V_HEAD_DIM), lambda h, i: (i, h)),
        out_shape=jax.ShapeDtypeStruct((seq, ATTN_WIDTH), _BF16),
        scratch_shapes=[
            pltpu.VMEM((2, 2 * HEAD_DIM, t), _BF16),
            pltpu.VMEM((2, 2 * HEAD_DIM, t), _BF16),
            pltpu.VMEM((2, t, t), _F32),
            pltpu.VMEM((2, t, t), _F32),
            pltpu.VMEM((2, t, t), _BF16),
            pltpu.VMEM((2, 1, t), _F32),
            pltpu.VMEM((2, V_AUG_DIM, t), _F32),
            pltpu.VMEM((2, 1, t), _F32),
        ],
        compiler_params=pltpu.CompilerParams(
            dimension_semantics=("arbitrary", "arbitrary"), vmem_limit_bytes=VMEM_LIMIT_BYTES),
        name="diff_attn",
    )(lam, qT, qT, k, vT, bias_vec, subln_g_col)


def _t5_bucket(rel):
    nb = NUM_BUCKETS // 2
    max_exact = nb // 2
    bucket = (rel > 0).astype(jnp.int32) * nb
    n = jnp.abs(rel)
    n_f = jnp.maximum(n, 1).astype(jnp.float32)
    large = max_exact + (jnp.log(n_f / max_exact) / math.log(MAX_DISTANCE / max_exact)
                         * (nb - max_exact)).astype(jnp.int32)
    large = jnp.minimum(large, nb - 1)
    return bucket + jnp.where(n < max_exact, n, large)


def _near_bias_vector(rel_table, t):
    assert t >= MAX_DISTANCE and t % CHUNK == 0
    rel = jnp.arange(3 * t - 1, dtype=jnp.int32) - (2 * t - 1)
    far = rel_table[_t5_bucket(jnp.int32(-(t + 1)))].astype(_F32)
    vec = (rel_table[_t5_bucket(rel)].astype(_F32) - far) * LOG2_E
    vec = jnp.pad(vec[::-1], ((0, 1), (0, 0)))
    return vec.T.reshape(rel_table.shape[1], 1, 3 * t)


def _mixer_out_kernel(x_ref, on_ref, ga_ref, gp_ref, w_pa_ref, w_out_ref, g_mlp_ref,
                      w_up_ref, w_down_ref, g_fin_ref, out_ref):
    y_attn = jnp.dot(on_ref[...], w_pa_ref[...], preferred_element_type=_F32)
    merged = ga_ref[...].astype(_F32) * y_attn + gp_ref[...]
    x1 = x_ref[...] + jnp.dot(merged.astype(_BF16), w_out_ref[...], preferred_element_type=_F32)
    h = _rms_norm(x1, g_mlp_ref[...], NORM_EPS).astype(_BF16)
    x2 = x1
    for c in range(0, D_FF, FF_CHUNK):
        up = jnp.dot(h, w_up_ref[:, c:c + FF_CHUNK], preferred_element_type=_F32)
        act = jnp.square(jnp.maximum(up, 0.0)).astype(_BF16)
        x2 = x2 + jnp.dot(act, w_down_ref[c:c + FF_CHUNK, :], preferred_element_type=_F32)
    out_ref[...] = _rms_norm(x2, g_fin_ref[...], NORM_EPS)


def _mixer_out(x2d, on, ga, gp, w_pa, w_out, g_mlp, w_up, w_down, g_fin):
    seq = x2d.shape[0]
    tm = SEQ_TILE
    row_tile = pl.BlockSpec((tm, D_MODEL), lambda i: (i, 0))
    return pl.pallas_call(
        _mixer_out_kernel,
        grid=(seq // tm,),
        in_specs=[
            row_tile, row_tile, row_tile, row_tile,
            _resident((ATTN_WIDTH, D_MODEL)),
            _resident((D_MODEL, D_MODEL)),
            _resident((1, D_MODEL)),
            _resident((D_MODEL, D_FF)),
            _resident((D_FF, D_MODEL)),
            _resident((1, D_MODEL)),
        ],
        out_specs=row_tile,
        out_shape=jax.ShapeDtypeStruct((seq, D_MODEL), _F32),
        compiler_params=pltpu.CompilerParams(
            dimension_semantics=("arbitrary",), vmem_limit_bytes=VMEM_LIMIT_BYTES),
        name="mixer_out",
    )(x2d, on, ga, gp, w_pa, w_out, g_mlp, w_up, w_down, g_fin)


def kernel(x, ln_mix_g, w_in, rel_bias_table, lambda_q1, lambda_k1, lambda_q2, lambda_k2,
           subln_g, w_proj_attn, w_pool_grp, pool_scale, w_proj_pool, w_out,
           ln_mlp_g, w_mlp_up, w_mlp_down, ln_final_g):
    batch, seq, d_model = x.shape
    assert batch == 1 and d_model == D_MODEL and seq % SEQ_TILE == 0
    assert ln_mix_g.shape[0] == 1, "single-layer trunk"
    x2d = x.reshape(seq, d_model)

    qT, k, vT, ga, gp = _mixer_in(
        x2d, ln_mix_g, w_in[0].astype(_BF16), w_pool_grp[0].astype(_BF16), pool_scale,
        w_proj_pool[0].astype(_BF16))

    lam = pl.pallas_call(
        _lambda_kernel, out_shape=jax.ShapeDtypeStruct((1, 1), _F32), name="diff_lambda",
    )(lambda_q1, lambda_k1, lambda_q2, lambda_k2)
    bias_vec = _near_bias_vector(rel_bias_table, SEQ_TILE)
    on = _diff_attn(lam, qT, k, vT, bias_vec, subln_g.reshape(V_HEAD_DIM, 1))

    out = _mixer_out(
        x2d, on, ga, gp, w_proj_attn[0].astype(_BF16), w_out[0].astype(_BF16), ln_mlp_g,
        w_mlp_up[0].astype(_BF16), w_mlp_down[0].astype(_BF16), ln_final_g.reshape(1, d_model))
    return out.reshape(batch, seq, d_model)
```

```python
import math

import jax
import jax.numpy as jnp
from jax import lax
from jax.experimental import pallas as pl
from jax.experimental.pallas import tpu as pltpu

D_MODEL = 1024
N_HEADS = 8
HEAD_DIM = 64
V_HEAD_DIM = 2 * HEAD_DIM
QK_WIDTH = N_HEADS * 2 * HEAD_DIM
ATTN_WIDTH = N_HEADS * V_HEAD_DIM
BF16_SUBLANES = 16
V_AUG_DIM = V_HEAD_DIM + BF16_SUBLANES
POOL_WINDOWS = (2, 4, 8, 16)
POOL_GROUP_WIDTH = D_MODEL // len(POOL_WINDOWS)
MAX_WINDOW = max(POOL_WINDOWS)
D_FF = 4 * D_MODEL
CHUNK = 64
NUM_BUCKETS = 32
MAX_DISTANCE = 128
NORM_EPS = 1e-6
SUBLN_EPS = 1e-5
NEG_INF = -1e30
LAMBDA_INIT = 0.8 - 0.6 * math.exp(-0.3 * 0)
LOG2_E = math.log2(math.e)

SEQ_TILE = 512
FF_CHUNK = 1024
VMEM_LIMIT_BYTES = 56 * 1024 * 1024

_BF16 = jnp.bfloat16
_F32 = jnp.float32


def _rms_norm(x, g, eps):
    return x * lax.rsqrt(jnp.mean(x * x, axis=-1, keepdims=True) + eps) * g


def _resident(shape):
    return pl.BlockSpec(shape, lambda *_: (0,) * len(shape), pipeline_mode=pl.Buffered(1))


def _mixer_in_kernel(x_ref, g_ref, w_in_ref, w_grp_ref, pscale_ref, w_pp_ref,
                     qT_ref, k_ref, vT_ref, ga_ref, gp_ref, ubuf_ref):
    i = pl.program_id(0)
    tm = x_ref.shape[0]
    h = _rms_norm(x_ref[...], g_ref[...], NORM_EPS).astype(_BF16)

    def proj(col, width):
        return jnp.dot(h, w_in_ref[:, col:col + width], preferred_element_type=_F32)

    q = proj(0, QK_WIDTH) * (HEAD_DIM ** -0.5 * LOG2_E)
    qT_ref[0] = q.T.astype(_BF16)
    k_ref[0] = proj(QK_WIDTH, QK_WIDTH).astype(_BF16)
    vT = proj(2 * QK_WIDTH, ATTN_WIDTH).T.astype(_BF16)
    ones = jnp.ones((BF16_SUBLANES, tm), _BF16)
    for hd in range(N_HEADS):
        vT_ref[0, hd * V_AUG_DIM:hd * V_AUG_DIM + V_HEAD_DIM, :] = (
            vT[hd * V_HEAD_DIM:(hd + 1) * V_HEAD_DIM, :])
        vT_ref[0, hd * V_AUG_DIM + V_HEAD_DIM:(hd + 1) * V_AUG_DIM, :] = ones

    @pl.when(i == 0)
    def _():
        ubuf_ref[0:MAX_WINDOW, :] = jnp.zeros((MAX_WINDOW, D_MODEL), _F32)

    ubuf_ref[MAX_WINDOW:MAX_WINDOW + tm, :] = proj(2 * QK_WIDTH + ATTN_WIDTH, D_MODEL)
    pos = i * tm + lax.broadcasted_iota(jnp.int32, (tm, 1), 0)
    y_pool = jnp.zeros((tm, D_MODEL), _F32)
    for g, w in enumerate(POOL_WINDOWS):
        cols = slice(g * POOL_GROUP_WIDTH, (g + 1) * POOL_GROUP_WIDTH)
        u = ubuf_ref[MAX_WINDOW:MAX_WINDOW + tm, cols]
        window_sum = u
        for s in range(1, w):
            window_sum = window_sum + ubuf_ref[MAX_WINDOW - s:MAX_WINDOW - s + tm, cols]
        count = jnp.minimum(pos + 1, w).astype(_F32)
        z = window_sum / count - u
        zg = jnp.dot(z.astype(_BF16), w_grp_ref[g], preferred_element_type=_F32)
        zg = zg * pscale_ref[:, cols]
        y_pool = y_pool + jnp.dot(zg.astype(_BF16), w_pp_ref[cols, :],
                                  preferred_element_type=_F32)
    ubuf_ref[0:MAX_WINDOW, :] = ubuf_ref[tm:tm + MAX_WINDOW, :]

    gate_col = 2 * QK_WIDTH + ATTN_WIDTH + D_MODEL
    ga_ref[...] = jax.nn.sigmoid(proj(gate_col, D_MODEL)).astype(_BF16)
    gp_ref[...] = jax.nn.sigmoid(proj(gate_col + D_MODEL, D_MODEL)) * y_pool


def _mixer_in(x2d, ln_g, w_in, w_grp, pool_scale, w_pp):
    seq = x2d.shape[0]
    tm = SEQ_TILE
    n_tiles = seq // tm
    in_width = w_in.shape[1]
    row_tile = lambda i: (i, 0)
    tile3 = lambda i: (i, 0, 0)
    return pl.pallas_call(
        _mixer_in_kernel,
        grid=(n_tiles,),
        in_specs=[
            pl.BlockSpec((tm, D_MODEL), row_tile),
            _resident((1, D_MODEL)),
            _resident((D_MODEL, in_width)),
            _resident(w_grp.shape),
            _resident((1, D_MODEL)),
            _resident((D_MODEL, D_MODEL)),
        ],
        out_specs=[
            pl.BlockSpec((1, QK_WIDTH, tm), tile3),
            pl.BlockSpec((1, tm, QK_WIDTH), tile3),
            pl.BlockSpec((1, N_HEADS * V_AUG_DIM, tm), tile3),
            pl.BlockSpec((tm, D_MODEL), row_tile),
            pl.BlockSpec((tm, D_MODEL), row_tile),
        ],
        out_shape=[
            jax.ShapeDtypeStruct((n_tiles, QK_WIDTH, tm), _BF16),
            jax.ShapeDtypeStruct((n_tiles, tm, QK_WIDTH), _BF16),
            jax.ShapeDtypeStruct((n_tiles, N_HEADS * V_AUG_DIM, tm), _BF16),
            jax.ShapeDtypeStruct((seq, D_MODEL), _BF16),
            jax.ShapeDtypeStruct((seq, D_MODEL), _F32),
        ],
        scratch_shapes=[pltpu.VMEM((tm + MAX_WINDOW, D_MODEL), _F32)],
        compiler_params=pltpu.CompilerParams(
            dimension_semantics=("arbitrary",), vmem_limit_bytes=VMEM_LIMIT_BYTES),
        name="mixer_in",
    )(x2d, ln_g, w_in, w_grp, pool_scale, w_pp)


def _lambda_kernel(lq1_ref, lk1_ref, lq2_ref, lk2_ref, lam_ref):
    a = jnp.sum(lq1_ref[...] * lk1_ref[...], axis=-1, keepdims=True)
    b = jnp.sum(lq2_ref[...] * lk2_ref[...], axis=-1, keepdims=True)
    lam_ref[...] = jnp.exp(a) - jnp.exp(b) + LAMBDA_INIT


def _diff_attn_kernel(lam_ref, qT_ref, qT_next_ref, k_ref, vT_ref, bvec_ref, g_ref, o_ref,
                      qpad_ref, qpad_next_ref, bias_ref, s_ref, p_ref, m_ref, acc_ref,
                      first_max_ref):
    i = pl.program_id(1)
    t = qT_ref.shape[2]
    n_vec = bvec_ref.shape[2]

    @pl.when(i == 0)
    def _():
        acc_ref[...] = jnp.zeros(acc_ref.shape, _F32)
        p_ref[...] = jnp.zeros(p_ref.shape, _BF16)
        rows = CHUNK
        for kind, c0 in ((0, 2 * t - 1), (1, t - 1)):
            for r0 in range(0 if kind == 1 else t - MAX_DISTANCE, t, rows):
                x = jnp.broadcast_to(bvec_ref[0], (rows, n_vec))
                x = pltpu.roll(x, (n_vec - c0 + r0) % n_vec, 1, stride=1, stride_axis=0)[:, :t]
                if kind == 1:
                    key = r0 + lax.broadcasted_iota(jnp.int32, (rows, t), 0)
                    qry = lax.broadcasted_iota(jnp.int32, (rows, t), 1)
                    x = jnp.where(key // CHUNK <= qry // CHUNK, x, NEG_INF)
                bias_ref[kind, r0:r0 + rows, :] = x

    for src_ref, dst_ref in ((qT_ref, qpad_ref), (qT_next_ref, qpad_next_ref)):
        qT = src_ref[0]
        row = lax.broadcasted_iota(jnp.int32, qT.shape, 0)
        zero = jnp.zeros_like(qT)
        dst_ref[0] = jnp.where(row < HEAD_DIM, qT, zero)
        dst_ref[1] = jnp.where(row >= HEAD_DIM, qT, zero)

    m_ref[...] = jnp.full(m_ref.shape, NEG_INF, _F32)
    no_rescale = (jnp.ones((1, t), _F32),) * 2

    def scores(tile, q_ref=qpad_ref):
        kt = k_ref[tile]
        tile_max = []
        for c in range(2):
            s = jnp.dot(kt, q_ref[c], preferred_element_type=_F32)
            s_ref[c] = s
            tile_max.append(jnp.max(s, axis=0, keepdims=True))
        return tuple(tile_max)

    def softmax(tile_max):
        alphas = []
        for c in range(2):
            m_old = m_ref[c]
            m_new = jnp.maximum(m_old, tile_max[c])
            p_ref[c] = jnp.exp2(s_ref[c] - m_new).astype(_BF16)
            m_ref[c] = m_new
            alphas.append(jnp.exp2(m_old - m_new))
        return tuple(alphas)

    def accumulate(tile, alphas):
        vt = vT_ref[jnp.maximum(tile, 0)]
        for c in range(2):
            acc_ref[c] = alphas[c] * acc_ref[c] + jnp.dot(vt, p_ref[c],
                                                          preferred_element_type=_F32)

    half = t // 2
    top, bot = slice(0, half), slice(half, t)

    def scores_diag(tile):
        kt = k_ref[tile]
        for c in range(2):
            s_ref[c, top, :] = jnp.dot(kt[top], qpad_ref[c], preferred_element_type=_F32)
            s_ref[c, bot, bot] = jnp.dot(kt[bot], qpad_ref[c, :, bot],
                                         preferred_element_type=_F32)

    def softmax_diag():
        alphas = []
        for c in range(2):
            s_top = s_ref[c, top, :] + bias_ref[1, top, :]
            s_bot = s_ref[c, bot, bot] + bias_ref[1, bot, bot]
            max_top = jnp.max(s_top, axis=0, keepdims=True)
            max_bot = jnp.max(s_bot, axis=0, keepdims=True)
            s_max = jnp.concatenate(
                [max_top[:, top], jnp.maximum(max_top[:, bot], max_bot)], axis=1)
            m_old = m_ref[c]
            m_new = jnp.maximum(m_old, s_max)
            p_ref[c, top, :] = jnp.exp2(s_top - m_new).astype(_BF16)
            p_ref[c, bot, bot] = jnp.exp2(s_bot - m_new[:, bot]).astype(_BF16)
            m_ref[c] = m_new
            alphas.append(jnp.exp2(m_old - m_new))
        return tuple(alphas)

    def accumulate_diag(tile, alphas):
        vt = vT_ref[tile]
        for c in range(2):
            acc_ref[c, :, top] = alphas[c][:, top] * acc_ref[c, :, top] + jnp.dot(
                vt[:, top], p_ref[c, top, top], preferred_element_type=_F32)
            acc_ref[c, :, bot] = alphas[c][:, bot] * acc_ref[c, :, bot] + jnp.dot(
                vt, p_ref[c, :, bot], preferred_element_type=_F32)

    def scores_for_next_step():
        for c, tile_max in enumerate(scores(0, qpad_next_ref)):
            first_max_ref[c] = tile_max

    n_far = jnp.maximum(i - 1, 0)

    @pl.when(i == 0)
    def _():
        for c, tile_max in enumerate(scores(0)):
            first_max_ref[c] = tile_max

    first_max = (first_max_ref[0], first_max_ref[1])

    def far_step(tile, carry):
        pending, tile_max = carry
        accumulate(tile - 1, pending)
        pending = softmax(tile_max)
        return pending, scores(tile + 1)

    def far_pair(jj, carry):
        return far_step(2 * jj + 1, far_step(2 * jj, carry))

    carry = lax.fori_loop(0, n_far // 2, far_pair, (no_rescale, first_max))
    pending, prev_max = lax.cond(n_far % 2 == 1, lambda c: far_step(n_far - 1, c), lambda c: c,
                                 carry)

    def add_corner_bias(tile_max):
        keys, qrys = slice(t - MAX_DISTANCE, t), slice(0, MAX_DISTANCE)
        lane = lax.broadcasted_iota(jnp.int32, (1, t), 1)
        biased_max = []
        for c in range(2):
            blk = s_ref[c, keys, qrys] + bias_ref[0, keys, qrys]
            s_ref[c, keys, qrys] = blk
            col_max = jnp.maximum(jnp.max(blk, axis=0, keepdims=True),
                                  jnp.max(s_ref[c, 0:t - MAX_DISTANCE, qrys], axis=0,
                                          keepdims=True))
            col_max = jnp.concatenate([col_max] * (t // MAX_DISTANCE), axis=1)
            biased_max.append(jnp.where(lane < MAX_DISTANCE, col_max, tile_max[c]))
        return tuple(biased_max)

    @pl.when(i == 0)
    def _():
        alphas = softmax_diag()
        scores_for_next_step()
        accumulate_diag(0, alphas)

    @pl.when(i > 0)
    def _():
        accumulate(i - 2, pending)
        alphas = softmax(add_corner_bias(prev_max))
        scores_diag(i)
        accumulate(i - 1, alphas)
        alphas = softmax_diag()
        scores_for_next_step()
        accumulate_diag(i, alphas)

    lam = lam_ref[0, 0]
    o = (acc_ref[0, :V_HEAD_DIM] / acc_ref[0, V_HEAD_DIM:V_HEAD_DIM + 1]
         - lam * (acc_ref[1, :V_HEAD_DIM] / acc_ref[1, V_HEAD_DIM:V_HEAD_DIM + 1]))
    ms = jnp.mean(o * o, axis=0, keepdims=True)
    o = o * lax.rsqrt(ms + SUBLN_EPS) * g_ref[...] * (1.0 - LAMBDA_INIT)
    o_ref[...] = o.T.astype(_BF16)


def _diff_attn(lam, qT, k, vT, bias_vec, subln_g_col):
    n_tiles, _, t = qT.shape
    seq = n_tiles * t
    n_vec = bias_vec.shape[2]
    return pl.pallas_call(
        _diff_attn_kernel,
        grid=(N_HEADS, n_tiles),
        in_specs=[
            pl.BlockSpec(memory_space=pltpu.SMEM),
            pl.BlockSpec((1, 2 * HEAD_DIM, t), lambda h, i: (i, h, 0)),
            pl.BlockSpec((1, 2 * HEAD_DIM, t),
                         lambda h, i: (jnp.minimum(i + 1, n_tiles - 1), h, 0)),
            pl.BlockSpec((n_tiles, t, 2 * HEAD_DIM), lambda h, i: (0, 0, h)),
            pl.BlockSpec((n_tiles, V_AUG_DIM, t), lambda h, i: (0, h, 0)),
            pl.BlockSpec((1, 1, n_vec), lambda h, i: (h, 0, 0)),
            pl.BlockSpec((V_HEAD_DIM, 1), lambda h, i: (0, 0)),
        ],
        out_specs=pl.BlockSpec((t, V_HEAD_DIM), lambda h, i: (i, h)),
        out_shape=jax.ShapeDtypeStruct((seq, ATTN_WIDTH), _BF16),
        scratch_shapes=[
            pltpu.VMEM((2, 2 * HEAD_DIM, t), _BF16),
            pltpu.VMEM((2, 2 * HEAD_DIM, t), _BF16),
            pltpu.VMEM((2, t, t), _F32),
            pltpu.VMEM((2, t, t), _F32),
            pltpu.VMEM((2, t, t), _BF16),
            pltpu.VMEM((2, 1, t), _F32),
            pltpu.VMEM((2, V_AUG_DIM, t), _F32),
            pltpu.VMEM((2, 1, t), _F32),
        ],
        compiler_params=pltpu.CompilerParams(
            dimension_semantics=("arbitrary", "arbitrary"), vmem_limit_bytes=VMEM_LIMIT_BYTES),
        name="diff_attn",
    )(lam, qT, qT, k, vT, bias_vec, subln_g_col)


def _t5_bucket(rel):
    nb = NUM_BUCKETS // 2
    max_exact = nb // 2
    bucket = (rel > 0).astype(jnp.int32) * nb
    n = jnp.abs(rel)
    n_f = jnp.maximum(n, 1).astype(jnp.float32)
    large = max_exact + (jnp.log(n_f / max_exact) / math.log(MAX_DISTANCE / max_exact)
                         * (nb - max_exact)).astype(jnp.int32)
    large = jnp.minimum(large, nb - 1)
    return bucket + jnp.where(n < max_exact, n, large)


def _near_bias_vector(rel_table, t):
    assert t >= MAX_DISTANCE and t % CHUNK == 0
    rel = jnp.arange(3 * t - 1, dtype=jnp.int32) - (2 * t - 1)
    far = rel_table[_t5_bucket(jnp.int32(-(t + 1)))].astype(_F32)
    vec = (rel_table[_t5_bucket(rel)].astype(_F32) - far) * LOG2_E
    vec = jnp.pad(vec[::-1], ((0, 1), (0, 0)))
    return vec.T.reshape(rel_table.shape[1], 1, 3 * t)


def _mixer_out_kernel(x_ref, on_ref, ga_ref, gp_ref, w_pa_ref, w_out_ref, g_mlp_ref,
                      w_up_ref, w_down_ref, g_fin_ref, out_ref):
    y_attn = jnp.dot(on_ref[...], w_pa_ref[...], preferred_element_type=_F32)
    merged = ga_ref[...].astype(_F32) * y_attn + gp_ref[...]
    x1 = x_ref[...] + jnp.dot(merged.astype(_BF16), w_out_ref[...], preferred_element_type=_F32)
    h = _rms_norm(x1, g_mlp_ref[...], NORM_EPS).astype(_BF16)
    x2 = x1
    for c in range(0, D_FF, FF_CHUNK):
        up = jnp.dot(h, w_up_ref[:, c:c + FF_CHUNK], preferred_element_type=_F32)
        act = jnp.square(jnp.maximum(up, 0.0)).astype(_BF16)
        x2 = x2 + jnp.dot(act, w_down_ref[c:c + FF_CHUNK, :], preferred_element_type=_F32)
    out_ref[...] = _rms_norm(x2, g_fin_ref[...], NORM_EPS)


def _mixer_out(x2d, on, ga, gp, w_pa, w_out, g_mlp, w_up, w_down, g_fin):
    seq = x2d.shape[0]
    tm = SEQ_TILE
    row_tile = pl.BlockSpec((tm, D_MODEL), lambda i: (i, 0))
    return pl.pallas_call(
        _mixer_out_kernel,
        grid=(seq // tm,),
        in_specs=[
            row_tile, row_tile, row_tile, row_tile,
            _resident((ATTN_WIDTH, D_MODEL)),
            _resident((D_MODEL, D_MODEL)),
            _resident((1, D_MODEL)),
            _resident((D_MODEL, D_FF)),
            _resident((D_FF, D_MODEL)),
            _resident((1, D_MODEL)),
        ],
        out_specs=row_tile,
        out_shape=jax.ShapeDtypeStruct((seq, D_MODEL), _F32),
        compiler_params=pltpu.CompilerParams(
            dimension_semantics=("arbitrary",), vmem_limit_bytes=VMEM_LIMIT_BYTES),
        name="mixer_out",
    )(x2d, on, ga, gp, w_pa, w_out, g_mlp, w_up, w_down, g_fin)


def kernel(x, ln_mix_g, w_in, rel_bias_table, lambda_q1, lambda_k1, lambda_q2, lambda_k2,
           subln_g, w_proj_attn, w_pool_grp, pool_scale, w_proj_pool, w_out,
           ln_mlp_g, w_mlp_up, w_mlp_down, ln_final_g):
    batch, seq, d_model = x.shape
    assert batch == 1 and d_model == D_MODEL and seq % SEQ_TILE == 0
    assert ln_mix_g.shape[0] == 1, "single-layer trunk"
    x2d = x.reshape(seq, d_model)

    qT, k, vT, ga, gp = _mixer_in(
        x2d, ln_mix_g, w_in[0].astype(_BF16), w_pool_grp[0].astype(_BF16), pool_scale,
        w_proj_pool[0].astype(_BF16))

    lam = pl.pallas_call(
        _lambda_kernel, out_shape=jax.ShapeDtypeStruct((1, 1), _F32), name="diff_lambda",
    )(lambda_q1, lambda_k1, lambda_q2, lambda_k2)
    bias_vec = _near_bias_vector(rel_bias_table, SEQ_TILE)
    on = _diff_attn(lam, qT, k, vT, bias_vec, subln_g.reshape(V_HEAD_DIM, 1))

    out = _mixer_out(
        x2d, on, ga, gp, w_proj_attn[0].astype(_BF16), w_out[0].astype(_BF16), ln_mlp_g,
        w_mlp_up[0].astype(_BF16), w_mlp_down[0].astype(_BF16), ln_final_g.reshape(1, d_model))
    return out.reshape(batch, seq, d_model)
```

```python
import math

import jax
import jax.numpy as jnp
from jax import lax
from jax.experimental import pallas as pl
from jax.experimental.pallas import tpu as pltpu

D_MODEL = 1024
N_HEADS = 8
HEAD_DIM = 64
V_HEAD_DIM = 2 * HEAD_DIM
QK_WIDTH = N_HEADS * 2 * HEAD_DIM
ATTN_WIDTH = N_HEADS * V_HEAD_DIM
BF16_SUBLANES = 16
V_AUG_DIM = V_HEAD_DIM + BF16_SUBLANES
POOL_WINDOWS = (2, 4, 8, 16)
POOL_GROUP_WIDTH = D_MODEL // len(POOL_WINDOWS)
MAX_WINDOW = max(POOL_WINDOWS)
D_FF = 4 * D_MODEL
CHUNK = 64
NUM_BUCKETS = 32
MAX_DISTANCE = 128
NORM_EPS = 1e-6
SUBLN_EPS = 1e-5
NEG_INF = -1e30
LAMBDA_INIT = 0.8 - 0.6 * math.exp(-0.3 * 0)
LOG2_E = math.log2(math.e)

SEQ_TILE = 512
FF_CHUNK = 1024
VMEM_LIMIT_BYTES = 56 * 1024 * 1024

_BF16 = jnp.bfloat16
_F32 = jnp.float32


def _rms_norm(x, g, eps):
    return x * lax.rsqrt(jnp.mean(x * x, axis=-1, keepdims=True) + eps) * g


def _resident(shape):
    return pl.BlockSpec(shape, lambda *_: (0,) * len(shape), pipeline_mode=pl.Buffered(1))


def _mixer_in_kernel(x_ref, g_ref, w_in_ref, w_grp_ref, pscale_ref, w_pp_ref,
                     qT_ref, k_ref, vT_ref, ga_ref, gp_ref, ubuf_ref):
    i = pl.program_id(0)
    tm = x_ref.shape[0]
    h = _rms_norm(x_ref[...], g_ref[...], NORM_EPS).astype(_BF16)

    def proj(col, width):
        return jnp.dot(h, w_in_ref[:, col:col + width], preferred_element_type=_F32)

    q = proj(0, QK_WIDTH) * (HEAD_DIM ** -0.5 * LOG2_E)
    qT_ref[0] = q.T.astype(_BF16)
    k_ref[0] = proj(QK_WIDTH, QK_WIDTH).astype(_BF16)
    vT = proj(2 * QK_WIDTH, ATTN_WIDTH).T.astype(_BF16)
    ones = jnp.ones((BF16_SUBLANES, tm), _BF16)
    for hd in range(N_HEADS):
        vT_ref[0, hd * V_AUG_DIM:hd * V_AUG_DIM + V_HEAD_DIM, :] = (
            vT[hd * V_HEAD_DIM:(hd + 1) * V_HEAD_DIM, :])
        vT_ref[0, hd * V_AUG_DIM + V_HEAD_DIM:(hd + 1) * V_AUG_DIM, :] = ones

    @pl.when(i == 0)
    def _():
        ubuf_ref[0:MAX_WINDOW, :] = jnp.zeros((MAX_WINDOW, D_MODEL), _F32)

    ubuf_ref[MAX_WINDOW:MAX_WINDOW + tm, :] = proj(2 * QK_WIDTH + ATTN_WIDTH, D_MODEL)
    pos = i * tm + lax.broadcasted_iota(jnp.int32, (tm, 1), 0)
    y_pool = jnp.zeros((tm, D_MODEL), _F32)
    for g, w in enumerate(POOL_WINDOWS):
        cols = slice(g * POOL_GROUP_WIDTH, (g + 1) * POOL_GROUP_WIDTH)
        u = ubuf_ref[MAX_WINDOW:MAX_WINDOW + tm, cols]
        window_sum = u
        for s in range(1, w):
            window_sum = window_sum + ubuf_ref[MAX_WINDOW - s:MAX_WINDOW - s + tm, cols]
        count = jnp.minimum(pos + 1, w).astype(_F32)
        z = window_sum / count - u
        zg = jnp.dot(z.astype(_BF16), w_grp_ref[g], preferred_element_type=_F32)
        zg = zg * pscale_ref[:, cols]
        y_pool = y_pool + jnp.dot(zg.astype(_BF16), w_pp_ref[cols, :],
                                  preferred_element_type=_F32)
    ubuf_ref[0:MAX_WINDOW, :] = ubuf_ref[tm:tm + MAX_WINDOW, :]

    gate_col = 2 * QK_WIDTH + ATTN_WIDTH + D_MODEL
    ga_ref[...] = jax.nn.sigmoid(proj(gate_col, D_MODEL)).astype(_BF16)
    gp_ref[...] = jax.nn.sigmoid(proj(gate_col + D_MODEL, D_MODEL)) * y_pool


def _mixer_in(x2d, ln_g, w_in, w_grp, pool_scale, w_pp):
    seq = x2d.shape[0]
    tm = SEQ_TILE
    n_tiles = seq // tm
    in_width = w_in.shape[1]
    row_tile = lambda i: (i, 0)
    tile3 = lambda i: (i, 0, 0)
    return pl.pallas_call(
        _mixer_in_kernel,
        grid=(n_tiles,),
        in_specs=[
            pl.BlockSpec((tm, D_MODEL), row_tile),
            _resident((1, D_MODEL)),
            _resident((D_MODEL, in_width)),
            _resident(w_grp.shape),
            _resident((1, D_MODEL)),
            _resident((D_MODEL, D_MODEL)),
        ],
        out_specs=[
            pl.BlockSpec((1, QK_WIDTH, tm), tile3),
            pl.BlockSpec((1, tm, QK_WIDTH), tile3),
            pl.BlockSpec((1, N_HEADS * V_AUG_DIM, tm), tile3),
            pl.BlockSpec((tm, D_MODEL), row_tile),
            pl.BlockSpec((tm, D_MODEL), row_tile),
        ],
        out_shape=[
            jax.ShapeDtypeStruct((n_tiles, QK_WIDTH, tm), _BF16),
            jax.ShapeDtypeStruct((n_tiles, tm, QK_WIDTH), _BF16),
            jax.ShapeDtypeStruct((n_tiles, N_HEADS * V_AUG_DIM, tm), _BF16),
            jax.ShapeDtypeStruct((seq, D_MODEL), _BF16),
            jax.ShapeDtypeStruct((seq, D_MODEL), _F32),
        ],
        scratch_shapes=[pltpu.VMEM((tm + MAX_WINDOW, D_MODEL), _F32)],
        compiler_params=pltpu.CompilerParams(
            dimension_semantics=("arbitrary",), vmem_limit_bytes=VMEM_LIMIT_BYTES),
        name="mixer_in",
    )(x2d, ln_g, w_in, w_grp, pool_scale, w_pp)


def _lambda_kernel(lq1_ref, lk1_ref, lq2_ref, lk2_ref, lam_ref):
    a = jnp.sum(lq1_ref[...] * lk1_ref[...], axis=-1, keepdims=True)
    b = jnp.sum(lq2_ref[...] * lk2_ref[...], axis=-1, keepdims=True)
    lam_ref[...] = jnp.exp(a) - jnp.exp(b) + LAMBDA_INIT


def _diff_attn_kernel(lam_ref, qT_ref, qT_next_ref, k_ref, vT_ref, bvec_ref, g_ref, o_ref,
                      qpad_ref, qpad_next_ref, bias_ref, s_ref, p_ref, m_ref, acc_ref,
                      first_max_ref):
    i = pl.program_id(1)
    t = qT_ref.shape[2]
    n_vec = bvec_ref.shape[2]

    @pl.when(i == 0)
    def _():
        acc_ref[...] = jnp.zeros(acc_ref.shape, _F32)
        p_ref[...] = jnp.zeros(p_ref.shape, _BF16)
        rows = CHUNK
        for kind, c0 in ((0, 2 * t - 1), (1, t - 1)):
            for r0 in range(0 if kind == 1 else t - MAX_DISTANCE, t, rows):
                x = jnp.broadcast_to(bvec_ref[0], (rows, n_vec))
                x = pltpu.roll(x, (n_vec - c0 + r0) % n_vec, 1, stride=1, stride_axis=0)[:, :t]
                if kind == 1:
                    key = r0 + lax.broadcasted_iota(jnp.int32, (rows, t), 0)
                    qry = lax.broadcasted_iota(jnp.int32, (rows, t), 1)
                    x = jnp.where(key // CHUNK <= qry // CHUNK, x, NEG_INF)
                bias_ref[kind, r0:r0 + rows, :] = x

    for src_ref, dst_ref in ((qT_ref, qpad_ref), (qT_next_ref, qpad_next_ref)):
        qT = src_ref[0]
        row = lax.broadcasted_iota(jnp.int32, qT.shape, 0)
        zero = jnp.zeros_like(qT)
        dst_ref[0] = jnp.where(row < HEAD_DIM, qT, zero)
        dst_ref[1] = jnp.where(row >= HEAD_DIM, qT, zero)

    m_ref[...] = jnp.full(m_ref.shape, NEG_INF, _F32)
    no_rescale = (jnp.ones((1, t), _F32),) * 2

    def scores(tile, q_ref=qpad_ref):
        kt = k_ref[tile]
        tile_max = []
        for c in range(2):
            s = jnp.dot(kt, q_ref[c], preferred_element_type=_F32)
            s_ref[c] = s
            tile_max.append(jnp.max(s, axis=0, keepdims=True))
        return tuple(tile_max)

    def softmax(tile_max):
        alphas = []
        for c in range(2):
            m_old = m_ref[c]
            m_new = jnp.maximum(m_old, tile_max[c])
            p_ref[c] = jnp.exp2(s_ref[c] - m_new).astype(_BF16)
            m_ref[c] = m_new
            alphas.append(jnp.exp2(m_old - m_new))
        return tuple(alphas)

    def accumulate(tile, alphas):
        vt = vT_ref[jnp.maximum(tile, 0)]
        for c in range(2):
            acc_ref[c] = alphas[c] * acc_ref[c] + jnp.dot(vt, p_ref[c],
                                                          preferred_element_type=_F32)

    half = t // 2
    top, bot = slice(0, half), slice(half, t)

    def scores_diag(tile):
        kt = k_ref[tile]
        for c in range(2):
            s_ref[c, top, :] = jnp.dot(kt[top], qpad_ref[c], preferred_element_type=_F32)
            s_ref[c, bot, bot] = jnp.dot(kt[bot], qpad_ref[c, :, bot],
                                         preferred_element_type=_F32)

    def softmax_diag():
        alphas = []
        for c in range(2):
            s_top = s_ref[c, top, :] + bias_ref[1, top, :]
            s_bot = s_ref[c, bot, bot] + bias_ref[1, bot, bot]
            max_top = jnp.max(s_top, axis=0, keepdims=True)
            max_bot = jnp.max(s_bot, axis=0, keepdims=True)
            s_max = jnp.concatenate(
                [max_top[:, top], jnp.maximum(max_top[:, bot], max_bot)], axis=1)
            m_old = m_ref[c]
            m_new = jnp.maximum(m_old, s_max)
            p_ref[c, top, :] = jnp.exp2(s_top - m_new).astype(_BF16)
            p_ref[c, bot, bot] = jnp.exp2(s_bot - m_new[:, bot]).astype(_BF16)
            m_ref[c] = m_new
            alphas.append(jnp.exp2(m_old - m_new))
        return tuple(alphas)

    def accumulate_diag(tile, alphas):
        vt = vT_ref[tile]
        for c in range(2):
            acc_ref[c, :, top] = alphas[c][:, top] * acc_ref[c, :, top] + jnp.dot(
                vt[:, top], p_ref[c, top, top], preferred_element_type=_F32)
            acc_ref[c, :, bot] = alphas[c][:, bot] * acc_ref[c, :, bot] + jnp.dot(
                vt, p_ref[c, :, bot], preferred_element_type=_F32)

    def scores_for_next_step():
        for c, tile_max in enumerate(scores(0, qpad_next_ref)):
            first_max_ref[c] = tile_max

    n_far = jnp.maximum(i - 1, 0)

    @pl.when(i == 0)
    def _():
        for c, tile_max in enumerate(scores(0)):
            first_max_ref[c] = tile_max

    first_max = (first_max_ref[0], first_max_ref[1])

    def far_step(tile, carry):
        pending, tile_max = carry
        accumulate(tile - 1, pending)
        pending = softmax(tile_max)
        return pending, scores(tile + 1)

    def far_pair(jj, carry):
        return far_step(2 * jj + 1, far_step(2 * jj, carry))

    carry = lax.fori_loop(0, n_far // 2, far_pair, (no_rescale, first_max))
    pending, prev_max = lax.cond(n_far % 2 == 1, lambda c: far_step(n_far - 1, c), lambda c: c,
                                 carry)

    def add_corner_bias(tile_max):
        keys, qrys = slice(t - MAX_DISTANCE, t), slice(0, MAX_DISTANCE)
        lane = lax.broadcasted_iota(jnp.int32, (1, t), 1)
        biased_max = []
        for c in range(2):
            blk = s_ref[c, keys, qrys] + bias_ref[0, keys, qrys]
            s_ref[c, keys, qrys] = blk
            col_max = jnp.maximum(jnp.max(blk, axis=0, keepdims=True),
                                  jnp.max(s_ref[c, 0:t - MAX_DISTANCE, qrys], axis=0,
                                          keepdims=True))
            col_max = jnp.concatenate([col_max] * (t // MAX_DISTANCE), axis=1)
            biased_max.append(jnp.where(lane < MAX_DISTANCE, col_max, tile_max[c]))
        return tuple(biased_max)

    @pl.when(i == 0)
    def _():
        alphas = softmax_diag()
        scores_for_next_step()
        accumulate_diag(0, alphas)

    @pl.when(i > 0)
    def _():
        accumulate(i - 2, pending)
        alphas = softmax(add_corner_bias(prev_max))
        scores_diag(i)
        accumulate(i - 1, alphas)
        alphas = softmax_diag()
        scores_for_next_step()
        accumulate_diag(i, alphas)

    lam = lam_ref[0, 0]
    o = (acc_ref[0, :V_HEAD_DIM] / acc_ref[0, V_HEAD_DIM:V_HEAD_DIM + 1]
         - lam * (acc_ref[1, :V_HEAD_DIM] / acc_ref[1, V_HEAD_DIM:V_HEAD_DIM + 1]))
    ms = jnp.mean(o * o, axis=0, keepdims=True)
    o = o * lax.rsqrt(ms + SUBLN_EPS) * g_ref[...] * (1.0 - LAMBDA_INIT)
    o_ref[...] = o.astype(_BF16)


def _diff_attn(lam, qT, k, vT, bias_vec, subln_g_col):
    n_tiles, _, t = qT.shape
    seq = n_tiles * t
    n_vec = bias_vec.shape[2]
    return pl.pallas_call(
        _diff_attn_kernel,
        grid=(N_HEADS, n_tiles),
        in_specs=[
            pl.BlockSpec(memory_space=pltpu.SMEM),
            pl.BlockSpec((1, 2 * HEAD_DIM, t), lambda h, i: (i, h, 0)),
            pl.BlockSpec((1, 2 * HEAD_DIM, t),
                         lambda h, i: (jnp.minimum(i + 1, n_tiles - 1), h, 0)),
            pl.BlockSpec((n_tiles, t, 2 * HEAD_DIM), lambda h, i: (0, 0, h)),
            pl.BlockSpec((n_tiles, V_AUG_DIM, t), lambda h, i: (0, h, 0)),
            pl.BlockSpec((1, 1, n_vec), lambda h, i: (h, 0, 0)),
            pl.BlockSpec((V_HEAD_DIM, 1), lambda h, i: (0, 0)),
        ],
        out_specs=pl.BlockSpec((V_HEAD_DIM, t), lambda h, i: (h, i)),
        out_shape=jax.ShapeDtypeStruct((ATTN_WIDTH, seq), _BF16),
        scratch_shapes=[
            pltpu.VMEM((2, 2 * HEAD_DIM, t), _BF16),
            pltpu.VMEM((2, 2 * HEAD_DIM, t), _BF16),
            pltpu.VMEM((2, t, t), _F32),
            pltpu.VMEM((2, t, t), _F32),
            pltpu.VMEM((2, t, t), _BF16),
            pltpu.VMEM((2, 1, t), _F32),
            pltpu.VMEM((2, V_AUG_DIM, t), _F32),
            pltpu.VMEM((2, 1, t), _F32),
        ],
        compiler_params=pltpu.CompilerParams(
            dimension_semantics=("arbitrary", "arbitrary"), vmem_limit_bytes=VMEM_LIMIT_BYTES),
        name="diff_attn",
    )(lam, qT, qT, k, vT, bias_vec, subln_g_col)


def _t5_bucket(rel):
    nb = NUM_BUCKETS // 2
    max_exact = nb // 2
    bucket = (rel > 0).astype(jnp.int32) * nb
    n = jnp.abs(rel)
    n_f = jnp.maximum(n, 1).astype(jnp.float32)
    large = max_exact + (jnp.log(n_f / max_exact) / math.log(MAX_DISTANCE / max_exact)
                         * (nb - max_exact)).astype(jnp.int32)
    large = jnp.minimum(large, nb - 1)
    return bucket + jnp.where(n < max_exact, n, large)


def _near_bias_vector(rel_table, t):
    assert t >= MAX_DISTANCE and t % CHUNK == 0
    rel = jnp.arange(3 * t - 1, dtype=jnp.int32) - (2 * t - 1)
    far = rel_table[_t5_bucket(jnp.int32(-(t + 1)))].astype(_F32)
    vec = (rel_table[_t5_bucket(rel)].astype(_F32) - far) * LOG2_E
    vec = jnp.pad(vec[::-1], ((0, 1), (0, 0)))
    return vec.T.reshape(rel_table.shape[1], 1, 3 * t)


def _mixer_out_kernel(x_ref, onT_ref, ga_ref, gp_ref, w_pa_ref, w_out_ref, g_mlp_ref,
                      w_up_ref, w_down_ref, g_fin_ref, out_ref):
    y_attn = lax.dot_general(onT_ref[...], w_pa_ref[...], (((0,), (0,)), ((), ())),
                             preferred_element_type=_F32)
    merged = ga_ref[...].astype(_F32) * y_attn + gp_ref[...]
    x1 = x_ref[...] + jnp.dot(merged.astype(_BF16), w_out_ref[...], preferred_element_type=_F32)
    h = _rms_norm(x1, g_mlp_ref[...], NORM_EPS).astype(_BF16)
    x2 = x1
    for c in range(0, D_FF, FF_CHUNK):
        up = jnp.dot(h, w_up_ref[:, c:c + FF_CHUNK], preferred_element_type=_F32)
        act = jnp.square(jnp.maximum(up, 0.0)).astype(_BF16)
        x2 = x2 + jnp.dot(act, w_down_ref[c:c + FF_CHUNK, :], preferred_element_type=_F32)
    out_ref[...] = _rms_norm(x2, g_fin_ref[...], NORM_EPS)


def _mixer_out(x2d, on, ga, gp, w_pa, w_out, g_mlp, w_up, w_down, g_fin):
    seq = x2d.shape[0]
    tm = SEQ_TILE
    row_tile = pl.BlockSpec((tm, D_MODEL), lambda i: (i, 0))
    return pl.pallas_call(
        _mixer_out_kernel,
        grid=(seq // tm,),
        in_specs=[
            row_tile, pl.BlockSpec((ATTN_WIDTH, tm), lambda i: (0, i)), row_tile, row_tile,
            _resident((ATTN_WIDTH, D_MODEL)),
            _resident((D_MODEL, D_MODEL)),
            _resident((1, D_MODEL)),
            _resident((D_MODEL, D_FF)),
            _resident((D_FF, D_MODEL)),
            _resident((1, D_MODEL)),
        ],
        out_specs=row_tile,
        out_shape=jax.ShapeDtypeStruct((seq, D_MODEL), _F32),
        compiler_params=pltpu.CompilerParams(
            dimension_semantics=("arbitrary",), vmem_limit_bytes=VMEM_LIMIT_BYTES),
        name="mixer_out",
    )(x2d, on, ga, gp, w_pa, w_out, g_mlp, w_up, w_down, g_fin)


def kernel(x, ln_mix_g, w_in, rel_bias_table, lambda_q1, lambda_k1, lambda_q2, lambda_k2,
           subln_g, w_proj_attn, w_pool_grp, pool_scale, w_proj_pool, w_out,
           ln_mlp_g, w_mlp_up, w_mlp_down, ln_final_g):
    batch, seq, d_model = x.shape
    assert batch == 1 and d_model == D_MODEL and seq % SEQ_TILE == 0
    assert ln_mix_g.shape[0] == 1, "single-layer trunk"
    x2d = x.reshape(seq, d_model)

    qT, k, vT, ga, gp = _mixer_in(
        x2d, ln_mix_g, w_in[0].astype(_BF16), w_pool_grp[0].astype(_BF16), pool_scale,
        w_proj_pool[0].astype(_BF16))

    lam = pl.pallas_call(
        _lambda_kernel, out_shape=jax.ShapeDtypeStruct((1, 1), _F32), name="diff_lambda",
    )(lambda_q1, lambda_k1, lambda_q2, lambda_k2)
    bias_vec = _near_bias_vector(rel_bias_table, SEQ_TILE)
    on = _diff_attn(lam, qT, k, vT, bias_vec, subln_g.reshape(V_HEAD_DIM, 1))

    out = _mixer_out(
        x2d, on, ga, gp, w_proj_attn[0].astype(_BF16), w_out[0].astype(_BF16), ln_mlp_g,
        w_mlp_up[0].astype(_BF16), w_mlp_down[0].astype(_BF16), ln_final_g.reshape(1, d_model))
    return out.reshape(batch, seq, d_model)
```

```python
import math

import jax
import jax.numpy as jnp
from jax import lax
from jax.experimental import pallas as pl
from jax.experimental.pallas import tpu as pltpu

D_MODEL = 1024
N_HEADS = 8
HEAD_DIM = 64
V_HEAD_DIM = 2 * HEAD_DIM
QK_WIDTH = N_HEADS * 2 * HEAD_DIM
ATTN_WIDTH = N_HEADS * V_HEAD_DIM
BF16_SUBLANES = 16
V_AUG_DIM = V_HEAD_DIM + BF16_SUBLANES
POOL_WINDOWS = (2, 4, 8, 16)
POOL_GROUP_WIDTH = D_MODEL // len(POOL_WINDOWS)
MAX_WINDOW = max(POOL_WINDOWS)
D_FF = 4 * D_MODEL
CHUNK = 64
NUM_BUCKETS = 32
MAX_DISTANCE = 128
NORM_EPS = 1e-6
SUBLN_EPS = 1e-5
NEG_INF = -1e30
LAMBDA_INIT = 0.8 - 0.6 * math.exp(-0.3 * 0)
LOG2_E = math.log2(math.e)

SEQ_TILE = 512
FF_CHUNK = 1024
VMEM_LIMIT_BYTES = 56 * 1024 * 1024

_BF16 = jnp.bfloat16
_F32 = jnp.float32


def _rms_norm(x, g, eps):
    return x * lax.rsqrt(jnp.mean(x * x, axis=-1, keepdims=True) + eps) * g


def _resident(shape):
    return pl.BlockSpec(shape, lambda *_: (0,) * len(shape), pipeline_mode=pl.Buffered(1))


def _mixer_in_kernel(x_ref, g_ref, w_in_ref, w_grp_ref, pscale_ref, w_pp_ref,
                     qT_ref, k_ref, vT_ref, ga_ref, gp_ref, ubuf_ref):
    i = pl.program_id(0)
    tm = x_ref.shape[0]
    h = _rms_norm(x_ref[...], g_ref[...], NORM_EPS).astype(_BF16)

    def proj(col, width):
        return jnp.dot(h, w_in_ref[:, col:col + width], preferred_element_type=_F32)

    q = proj(0, QK_WIDTH) * (HEAD_DIM ** -0.5 * LOG2_E)
    qT = q.T.astype(_BF16)
    zeros = jnp.zeros((HEAD_DIM, tm), _BF16)
    for hd in range(N_HEADS):
        for c in range(2):
            own = qT[(2 * hd + c) * HEAD_DIM:(2 * hd + c + 1) * HEAD_DIM, :]
            base = (2 * hd + c) * 2 * HEAD_DIM
            qT_ref[0, base:base + HEAD_DIM, :] = zeros if c else own
            qT_ref[0, base + HEAD_DIM:base + 2 * HEAD_DIM, :] = own if c else zeros
    k_ref[0] = proj(QK_WIDTH, QK_WIDTH).astype(_BF16)
    vT = proj(2 * QK_WIDTH, ATTN_WIDTH).T.astype(_BF16)
    ones = jnp.ones((BF16_SUBLANES, tm), _BF16)
    for hd in range(N_HEADS):
        vT_ref[0, hd * V_AUG_DIM:hd * V_AUG_DIM + V_HEAD_DIM, :] = (
            vT[hd * V_HEAD_DIM:(hd + 1) * V_HEAD_DIM, :])
        vT_ref[0, hd * V_AUG_DIM + V_HEAD_DIM:(hd + 1) * V_AUG_DIM, :] = ones

    @pl.when(i == 0)
    def _():
        ubuf_ref[0:MAX_WINDOW, :] = jnp.zeros((MAX_WINDOW, D_MODEL), _F32)

    ubuf_ref[MAX_WINDOW:MAX_WINDOW + tm, :] = proj(2 * QK_WIDTH + ATTN_WIDTH, D_MODEL)
    pos = i * tm + lax.broadcasted_iota(jnp.int32, (tm, 1), 0)
    y_pool = jnp.zeros((tm, D_MODEL), _F32)
    for g, w in enumerate(POOL_WINDOWS):
        cols = slice(g * POOL_GROUP_WIDTH, (g + 1) * POOL_GROUP_WIDTH)
        u = ubuf_ref[MAX_WINDOW:MAX_WINDOW + tm, cols]
        window_sum = u
        for s in range(1, w):
            window_sum = window_sum + ubuf_ref[MAX_WINDOW - s:MAX_WINDOW - s + tm, cols]
        count = jnp.minimum(pos + 1, w).astype(_F32)
        z = window_sum / count - u
        zg = jnp.dot(z.astype(_BF16), w_grp_ref[g], preferred_element_type=_F32)
        zg = zg * pscale_ref[:, cols]
        y_pool = y_pool + jnp.dot(zg.astype(_BF16), w_pp_ref[cols, :],
                                  preferred_element_type=_F32)
    ubuf_ref[0:MAX_WINDOW, :] = ubuf_ref[tm:tm + MAX_WINDOW, :]

    gate_col = 2 * QK_WIDTH + ATTN_WIDTH + D_MODEL
    ga_ref[...] = jax.nn.sigmoid(proj(gate_col, D_MODEL)).astype(_BF16)
    gp_ref[...] = jax.nn.sigmoid(proj(gate_col + D_MODEL, D_MODEL)) * y_pool


def _mixer_in(x2d, ln_g, w_in, w_grp, pool_scale, w_pp):
    seq = x2d.shape[0]
    tm = SEQ_TILE
    n_tiles = seq // tm
    in_width = w_in.shape[1]
    row_tile = lambda i: (i, 0)
    tile3 = lambda i: (i, 0, 0)
    return pl.pallas_call(
        _mixer_in_kernel,
        grid=(n_tiles,),
        in_specs=[
            pl.BlockSpec((tm, D_MODEL), row_tile),
            _resident((1, D_MODEL)),
            _resident((D_MODEL, in_width)),
            _resident(w_grp.shape),
            _resident((1, D_MODEL)),
            _resident((D_MODEL, D_MODEL)),
        ],
        out_specs=[
            pl.BlockSpec((1, 2 * QK_WIDTH, tm), tile3),
            pl.BlockSpec((1, tm, QK_WIDTH), tile3),
            pl.BlockSpec((1, N_HEADS * V_AUG_DIM, tm), tile3),
            pl.BlockSpec((tm, D_MODEL), row_tile),
            pl.BlockSpec((tm, D_MODEL), row_tile),
        ],
        out_shape=[
            jax.ShapeDtypeStruct((n_tiles, 2 * QK_WIDTH, tm), _BF16),
            jax.ShapeDtypeStruct((n_tiles, tm, QK_WIDTH), _BF16),
            jax.ShapeDtypeStruct((n_tiles, N_HEADS * V_AUG_DIM, tm), _BF16),
            jax.ShapeDtypeStruct((seq, D_MODEL), _BF16),
            jax.ShapeDtypeStruct((seq, D_MODEL), _F32),
        ],
        scratch_shapes=[pltpu.VMEM((tm + MAX_WINDOW, D_MODEL), _F32)],
        compiler_params=pltpu.CompilerParams(
            dimension_semantics=("arbitrary",), vmem_limit_bytes=VMEM_LIMIT_BYTES),
        name="mixer_in",
    )(x2d, ln_g, w_in, w_grp, pool_scale, w_pp)


def _lambda_kernel(lq1_ref, lk1_ref, lq2_ref, lk2_ref, lam_ref):
    a = jnp.sum(lq1_ref[...] * lk1_ref[...], axis=-1, keepdims=True)
    b = jnp.sum(lq2_ref[...] * lk2_ref[...], axis=-1, keepdims=True)
    lam_ref[...] = jnp.exp(a) - jnp.exp(b) + LAMBDA_INIT


def _diff_attn_kernel(lam_ref, qpad_ref, qpad_next_ref, k_ref, vT_ref, bvec_ref, g_ref, o_ref,
                      bias_ref, s_ref, p_ref, m_ref, acc_ref,
                      first_max_ref):
    i = pl.program_id(1)
    t = qpad_ref.shape[3]
    n_vec = bvec_ref.shape[2]

    @pl.when(i == 0)
    def _():
        acc_ref[...] = jnp.zeros(acc_ref.shape, _F32)
        p_ref[...] = jnp.zeros(p_ref.shape, _BF16)
        rows = CHUNK
        for kind, c0 in ((0, 2 * t - 1), (1, t - 1)):
            for r0 in range(0 if kind == 1 else t - MAX_DISTANCE, t, rows):
                x = jnp.broadcast_to(bvec_ref[0], (rows, n_vec))
                x = pltpu.roll(x, (n_vec - c0 + r0) % n_vec, 1, stride=1, stride_axis=0)[:, :t]
                if kind == 1:
                    key = r0 + lax.broadcasted_iota(jnp.int32, (rows, t), 0)
                    qry = lax.broadcasted_iota(jnp.int32, (rows, t), 1)
                    x = jnp.where(key // CHUNK <= qry // CHUNK, x, NEG_INF)
                bias_ref[kind, r0:r0 + rows, :] = x

    m_ref[...] = jnp.full(m_ref.shape, NEG_INF, _F32)
    no_rescale = (jnp.ones((1, t), _F32),) * 2

    def scores(tile, q_ref=qpad_ref.at[0]):
        kt = k_ref[tile]
        tile_max = []
        for c in range(2):
            s = jnp.dot(kt, q_ref[c], preferred_element_type=_F32)
            s_ref[c] = s
            tile_max.append(jnp.max(s, axis=0, keepdims=True))
        return tuple(tile_max)

    def softmax(tile_max):
        alphas = []
        for c in range(2):
            m_old = m_ref[c]
            m_new = jnp.maximum(m_old, tile_max[c])
            p_ref[c] = jnp.exp2(s_ref[c] - m_new).astype(_BF16)
            m_ref[c] = m_new
            alphas.append(jnp.exp2(m_old - m_new))
        return tuple(alphas)

    def accumulate(tile, alphas):
        vt = vT_ref[jnp.maximum(tile, 0)]
        for c in range(2):
            acc_ref[c] = alphas[c] * acc_ref[c] + jnp.dot(vt, p_ref[c],
                                                          preferred_element_type=_F32)

    half = t // 2
    top, bot = slice(0, half), slice(half, t)

    def scores_diag(tile):
        kt = k_ref[tile]
        for c in range(2):
            s_ref[c, top, :] = jnp.dot(kt[top], qpad_ref[0, c], preferred_element_type=_F32)
            s_ref[c, bot, bot] = jnp.dot(kt[bot], qpad_ref[0, c, :, bot],
                                         preferred_element_type=_F32)

    def softmax_diag():
        alphas = []
        for c in range(2):
            s_top = s_ref[c, top, :] + bias_ref[1, top, :]
            s_bot = s_ref[c, bot, bot] + bias_ref[1, bot, bot]
            max_top = jnp.max(s_top, axis=0, keepdims=True)
            max_bot = jnp.max(s_bot, axis=0, keepdims=True)
            s_max = jnp.concatenate(
                [max_top[:, top], jnp.maximum(max_top[:, bot], max_bot)], axis=1)
            m_old = m_ref[c]
            m_new = jnp.maximum(m_old, s_max)
            p_ref[c, top, :] = jnp.exp2(s_top - m_new).astype(_BF16)
            p_ref[c, bot, bot] = jnp.exp2(s_bot - m_new[:, bot]).astype(_BF16)
            m_ref[c] = m_new
            alphas.append(jnp.exp2(m_old - m_new))
        return tuple(alphas)

    def accumulate_diag(tile, alphas):
        vt = vT_ref[tile]
        for c in range(2):
            acc_ref[c, :, top] = alphas[c][:, top] * acc_ref[c, :, top] + jnp.dot(
                vt[:, top], p_ref[c, top, top], preferred_element_type=_F32)
            acc_ref[c, :, bot] = alphas[c][:, bot] * acc_ref[c, :, bot] + jnp.dot(
                vt, p_ref[c, :, bot], preferred_element_type=_F32)

    def scores_for_next_step():
        for c, tile_max in enumerate(scores(0, qpad_next_ref.at[0])):
            first_max_ref[c] = tile_max

    n_far = jnp.maximum(i - 1, 0)

    @pl.when(i == 0)
    def _():
        for c, tile_max in enumerate(scores(0)):
            first_max_ref[c] = tile_max

    first_max = (first_max_ref[0], first_max_ref[1])

    def far_step(tile, carry):
        pending, tile_max = carry
        accumulate(tile - 1, pending)
        pending = softmax(tile_max)
        return pending, scores(tile + 1)

    def far_pair(jj, carry):
        return far_step(2 * jj + 1, far_step(2 * jj, carry))

    carry = lax.fori_loop(0, n_far // 2, far_pair, (no_rescale, first_max))
    pending, prev_max = lax.cond(n_far % 2 == 1, lambda c: far_step(n_far - 1, c), lambda c: c,
                                 carry)

    def add_corner_bias(tile_max):
        keys, qrys = slice(t - MAX_DISTANCE, t), slice(0, MAX_DISTANCE)
        lane = lax.broadcasted_iota(jnp.int32, (1, t), 1)
        biased_max = []
        for c in range(2):
            blk = s_ref[c, keys, qrys] + bias_ref[0, keys, qrys]
            s_ref[c, keys, qrys] = blk
            col_max = jnp.maximum(jnp.max(blk, axis=0, keepdims=True),
                                  jnp.max(s_ref[c, 0:t - MAX_DISTANCE, qrys], axis=0,
                                          keepdims=True))
            col_max = jnp.concatenate([col_max] * (t // MAX_DISTANCE), axis=1)
            biased_max.append(jnp.where(lane < MAX_DISTANCE, col_max, tile_max[c]))
        return tuple(biased_max)

    @pl.when(i == 0)
    def _():
        alphas = softmax_diag()
        scores_for_next_step()
        accumulate_diag(0, alphas)

    @pl.when(i > 0)
    def _():
        accumulate(i - 2, pending)
        alphas = softmax(add_corner_bias(prev_max))
        scores_diag(i)
        accumulate(i - 1, alphas)
        alphas = softmax_diag()
        scores_for_next_step()
        accumulate_diag(i, alphas)

    lam = lam_ref[0, 0]
    o = (acc_ref[0, :V_HEAD_DIM] / acc_ref[0, V_HEAD_DIM:V_HEAD_DIM + 1]
         - lam * (acc_ref[1, :V_HEAD_DIM] / acc_ref[1, V_HEAD_DIM:V_HEAD_DIM + 1]))
    ms = jnp.mean(o * o, axis=0, keepdims=True)
    o = o * lax.rsqrt(ms + SUBLN_EPS) * g_ref[...] * (1.0 - LAMBDA_INIT)
    o_ref[...] = o.astype(_BF16)


def _diff_attn(lam, qT, k, vT, bias_vec, subln_g_col):
    n_tiles, _, _, t = qT.shape
    seq = n_tiles * t
    n_vec = bias_vec.shape[2]
    return pl.pallas_call(
        _diff_attn_kernel,
        grid=(N_HEADS, n_tiles),
        in_specs=[
            pl.BlockSpec(memory_space=pltpu.SMEM),
            pl.BlockSpec((1, 2, 2 * HEAD_DIM, t), lambda h, i: (i, h, 0, 0)),
            pl.BlockSpec((1, 2, 2 * HEAD_DIM, t),
                         lambda h, i: (jnp.minimum(i + 1, n_tiles - 1), h, 0, 0)),
            pl.BlockSpec((n_tiles, t, 2 * HEAD_DIM), lambda h, i: (0, 0, h)),
            pl.BlockSpec((n_tiles, V_AUG_DIM, t), lambda h, i: (0, h, 0)),
            pl.BlockSpec((1, 1, n_vec), lambda h, i: (h, 0, 0)),
            pl.BlockSpec((V_HEAD_DIM, 1), lambda h, i: (0, 0)),
        ],
        out_specs=pl.BlockSpec((V_HEAD_DIM, t), lambda h, i: (h, i)),
        out_shape=jax.ShapeDtypeStruct((ATTN_WIDTH, seq), _BF16),
        scratch_shapes=[
            pltpu.VMEM((2, t, t), _F32),
            pltpu.VMEM((2, t, t), _F32),
            pltpu.VMEM((2, t, t), _BF16),
            pltpu.VMEM((2, 1, t), _F32),
            pltpu.VMEM((2, V_AUG_DIM, t), _F32),
            pltpu.VMEM((2, 1, t), _F32),
        ],
        compiler_params=pltpu.CompilerParams(
            dimension_semantics=("arbitrary", "arbitrary"), vmem_limit_bytes=VMEM_LIMIT_BYTES),
        name="diff_attn",
    )(lam, qT, qT, k, vT, bias_vec, subln_g_col)


def _t5_bucket(rel):
    nb = NUM_BUCKETS // 2
    max_exact = nb // 2
    bucket = (rel > 0).astype(jnp.int32) * nb
    n = jnp.abs(rel)
    n_f = jnp.maximum(n, 1).astype(jnp.float32)
    large = max_exact + (jnp.log(n_f / max_exact) / math.log(MAX_DISTANCE / max_exact)
                         * (nb - max_exact)).astype(jnp.int32)
    large = jnp.minimum(large, nb - 1)
    return bucket + jnp.where(n < max_exact, n, large)


def _near_bias_vector(rel_table, t):
    assert t >= MAX_DISTANCE and t % CHUNK == 0
    rel = jnp.arange(3 * t - 1, dtype=jnp.int32) - (2 * t - 1)
    far = rel_table[_t5_bucket(jnp.int32(-(t + 1)))].astype(_F32)
    vec = (rel_table[_t5_bucket(rel)].astype(_F32) - far) * LOG2_E
    vec = jnp.pad(vec[::-1], ((0, 1), (0, 0)))
    return vec.T.reshape(rel_table.shape[1], 1, 3 * t)


def _mixer_out_kernel(x_ref, onT_ref, ga_ref, gp_ref, w_pa_ref, w_out_ref, g_mlp_ref,
                      w_up_ref, w_down_ref, g_fin_ref, out_ref):
    y_attn = lax.dot_general(onT_ref[...], w_pa_ref[...], (((0,), (0,)), ((), ())),
                             preferred_element_type=_F32)
    merged = ga_ref[...].astype(_F32) * y_attn + gp_ref[...]
    x1 = x_ref[...] + jnp.dot(merged.astype(_BF16), w_out_ref[...], preferred_element_type=_F32)
    h = _rms_norm(x1, g_mlp_ref[...], NORM_EPS).astype(_BF16)
    x2 = x1
    for c in range(0, D_FF, FF_CHUNK):
        up = jnp.dot(h, w_up_ref[:, c:c + FF_CHUNK], preferred_element_type=_F32)
        act = jnp.square(jnp.maximum(up, 0.0)).astype(_BF16)
        x2 = x2 + jnp.dot(act, w_down_ref[c:c + FF_CHUNK, :], preferred_element_type=_F32)
    out_ref[...] = _rms_norm(x2, g_fin_ref[...], NORM_EPS)


def _mixer_out(x2d, on, ga, gp, w_pa, w_out, g_mlp, w_up, w_down, g_fin):
    seq = x2d.shape[0]
    tm = SEQ_TILE
    row_tile = pl.BlockSpec((tm, D_MODEL), lambda i: (i, 0))
    return pl.pallas_call(
        _mixer_out_kernel,
        grid=(seq // tm,),
        in_specs=[
            row_tile, pl.BlockSpec((ATTN_WIDTH, tm), lambda i: (0, i)), row_tile, row_tile,
            _resident((ATTN_WIDTH, D_MODEL)),
            _resident((D_MODEL, D_MODEL)),
            _resident((1, D_MODEL)),
            _resident((D_MODEL, D_FF)),
            _resident((D_FF, D_MODEL)),
            _resident((1, D_MODEL)),
        ],
        out_specs=row_tile,
        out_shape=jax.ShapeDtypeStruct((seq, D_MODEL), _F32),
        compiler_params=pltpu.CompilerParams(
            dimension_semantics=("arbitrary",), vmem_limit_bytes=VMEM_LIMIT_BYTES),
        name="mixer_out",
    )(x2d, on, ga, gp, w_pa, w_out, g_mlp, w_up, w_down, g_fin)


def kernel(x, ln_mix_g, w_in, rel_bias_table, lambda_q1, lambda_k1, lambda_q2, lambda_k2,
           subln_g, w_proj_attn, w_pool_grp, pool_scale, w_proj_pool, w_out,
           ln_mlp_g, w_mlp_up, w_mlp_down, ln_final_g):
    batch, seq, d_model = x.shape
    assert batch == 1 and d_model == D_MODEL and seq % SEQ_TILE == 0
    assert ln_mix_g.shape[0] == 1, "single-layer trunk"
    x2d = x.reshape(seq, d_model)

    qT, k, vT, ga, gp = _mixer_in(
        x2d, ln_mix_g, w_in[0].astype(_BF16), w_pool_grp[0].astype(_BF16), pool_scale,
        w_proj_pool[0].astype(_BF16))

    lam = pl.pallas_call(
        _lambda_kernel, out_shape=jax.ShapeDtypeStruct((1, 1), _F32), name="diff_lambda",
    )(lambda_q1, lambda_k1, lambda_q2, lambda_k2)
    bias_vec = _near_bias_vector(rel_bias_table, SEQ_TILE)
    qT = qT.reshape(qT.shape[0], 2 * N_HEADS, 2 * HEAD_DIM, SEQ_TILE)
    on = _diff_attn(lam, qT, k, vT, bias_vec, subln_g.reshape(V_HEAD_DIM, 1))

    out = _mixer_out(
        x2d, on, ga, gp, w_proj_attn[0].astype(_BF16), w_out[0].astype(_BF16), ln_mlp_g,
        w_mlp_up[0].astype(_BF16), w_mlp_down[0].astype(_BF16), ln_final_g.reshape(1, d_model))
    return out.reshape(batch, seq, d_model)
```

```python
import itertools
import math

import jax
import jax.numpy as jnp
from jax import lax
from jax.experimental import pallas as pl
from jax.experimental.pallas import tpu as pltpu

D_MODEL = 1024
N_HEADS = 8
HEAD_DIM = 64
V_HEAD_DIM = 2 * HEAD_DIM
QK_WIDTH = N_HEADS * 2 * HEAD_DIM
ATTN_WIDTH = N_HEADS * V_HEAD_DIM
BF16_SUBLANES = 16
V_AUG_DIM = V_HEAD_DIM + BF16_SUBLANES
POOL_WINDOWS = (2, 4, 8, 16)
POOL_GROUP_WIDTH = D_MODEL // len(POOL_WINDOWS)
MAX_WINDOW = max(POOL_WINDOWS)
D_FF = 4 * D_MODEL
CHUNK = 64
NUM_BUCKETS = 32
MAX_DISTANCE = 128
NORM_EPS = 1e-6
SUBLN_EPS = 1e-5
NEG_INF = -1e30
LAMBDA_INIT = 0.8 - 0.6 * math.exp(-0.3 * 0)
LOG2_E = math.log2(math.e)

SEQ_TILE = 512
FF_CHUNK = 1024
HEADS_PER_STEP = 2
VMEM_LIMIT_BYTES = 56 * 1024 * 1024

_BF16 = jnp.bfloat16
_F32 = jnp.float32


def _rms_norm(x, g, eps):
    return x * lax.rsqrt(jnp.mean(x * x, axis=-1, keepdims=True) + eps) * g


def _resident(shape):
    return pl.BlockSpec(shape, lambda *_: (0,) * len(shape), pipeline_mode=pl.Buffered(1))


def _mixer_in_kernel(x_ref, g_ref, w_in_ref, w_grp_ref, pscale_ref, w_pp_ref,
                     qT_ref, k_ref, vT_ref, ga_ref, gp_ref, ubuf_ref):
    i = pl.program_id(0)
    tm = x_ref.shape[0]
    h = _rms_norm(x_ref[...], g_ref[...], NORM_EPS).astype(_BF16)

    def proj(col, width):
        return jnp.dot(h, w_in_ref[:, col:col + width], preferred_element_type=_F32)

    q = proj(0, QK_WIDTH) * (HEAD_DIM ** -0.5 * LOG2_E)
    qT_ref[0] = q.T.astype(_BF16)
    k_ref[0] = proj(QK_WIDTH, QK_WIDTH).astype(_BF16)
    vT = proj(2 * QK_WIDTH, ATTN_WIDTH).T.astype(_BF16)
    ones = jnp.ones((BF16_SUBLANES, tm), _BF16)
    for hd in range(N_HEADS):
        vT_ref[0, hd * V_AUG_DIM:hd * V_AUG_DIM + V_HEAD_DIM, :] = (
            vT[hd * V_HEAD_DIM:(hd + 1) * V_HEAD_DIM, :])
        vT_ref[0, hd * V_AUG_DIM + V_HEAD_DIM:(hd + 1) * V_AUG_DIM, :] = ones

    @pl.when(i == 0)
    def _():
        ubuf_ref[0:MAX_WINDOW, :] = jnp.zeros((MAX_WINDOW, D_MODEL), _F32)

    ubuf_ref[MAX_WINDOW:MAX_WINDOW + tm, :] = proj(2 * QK_WIDTH + ATTN_WIDTH, D_MODEL)
    pos = i * tm + lax.broadcasted_iota(jnp.int32, (tm, 1), 0)
    y_pool = jnp.zeros((tm, D_MODEL), _F32)
    for g, w in enumerate(POOL_WINDOWS):
        cols = slice(g * POOL_GROUP_WIDTH, (g + 1) * POOL_GROUP_WIDTH)
        u = ubuf_ref[MAX_WINDOW:MAX_WINDOW + tm, cols]
        window_sum = u
        for s in range(1, w):
            window_sum = window_sum + ubuf_ref[MAX_WINDOW - s:MAX_WINDOW - s + tm, cols]
        count = jnp.minimum(pos + 1, w).astype(_F32)
        z = window_sum / count - u
        zg = jnp.dot(z.astype(_BF16), w_grp_ref[g], preferred_element_type=_F32)
        zg = zg * pscale_ref[:, cols]
        y_pool = y_pool + jnp.dot(zg.astype(_BF16), w_pp_ref[cols, :],
                                  preferred_element_type=_F32)
    ubuf_ref[0:MAX_WINDOW, :] = ubuf_ref[tm:tm + MAX_WINDOW, :]

    gate_col = 2 * QK_WIDTH + ATTN_WIDTH + D_MODEL
    ga_ref[...] = jax.nn.sigmoid(proj(gate_col, D_MODEL)).astype(_BF16)
    gp_ref[...] = jax.nn.sigmoid(proj(gate_col + D_MODEL, D_MODEL)) * y_pool


def _mixer_in(x2d, ln_g, w_in, w_grp, pool_scale, w_pp):
    seq = x2d.shape[0]
    tm = SEQ_TILE
    n_tiles = seq // tm
    in_width = w_in.shape[1]
    row_tile = lambda i: (i, 0)
    tile3 = lambda i: (i, 0, 0)
    return pl.pallas_call(
        _mixer_in_kernel,
        grid=(n_tiles,),
        in_specs=[
            pl.BlockSpec((tm, D_MODEL), row_tile),
            _resident((1, D_MODEL)),
            _resident((D_MODEL, in_width)),
            _resident(w_grp.shape),
            _resident((1, D_MODEL)),
            _resident((D_MODEL, D_MODEL)),
        ],
        out_specs=[
            pl.BlockSpec((1, QK_WIDTH, tm), tile3),
            pl.BlockSpec((1, tm, QK_WIDTH), tile3),
            pl.BlockSpec((1, N_HEADS * V_AUG_DIM, tm), tile3),
            pl.BlockSpec((tm, D_MODEL), row_tile),
            pl.BlockSpec((tm, D_MODEL), row_tile),
        ],
        out_shape=[
            jax.ShapeDtypeStruct((n_tiles, QK_WIDTH, tm), _BF16),
            jax.ShapeDtypeStruct((n_tiles, tm, QK_WIDTH), _BF16),
            jax.ShapeDtypeStruct((n_tiles, N_HEADS * V_AUG_DIM, tm), _BF16),
            jax.ShapeDtypeStruct((seq, D_MODEL), _BF16),
            jax.ShapeDtypeStruct((seq, D_MODEL), _F32),
        ],
        scratch_shapes=[pltpu.VMEM((tm + MAX_WINDOW, D_MODEL), _F32)],
        compiler_params=pltpu.CompilerParams(
            dimension_semantics=("arbitrary",), vmem_limit_bytes=VMEM_LIMIT_BYTES),
        name="mixer_in",
    )(x2d, ln_g, w_in, w_grp, pool_scale, w_pp)


def _lambda_kernel(lq1_ref, lk1_ref, lq2_ref, lk2_ref, lam_ref):
    a = jnp.sum(lq1_ref[...] * lk1_ref[...], axis=-1, keepdims=True)
    b = jnp.sum(lq2_ref[...] * lk2_ref[...], axis=-1, keepdims=True)
    lam_ref[...] = jnp.exp(a) - jnp.exp(b) + LAMBDA_INIT


def _diff_attn_kernel(lam_ref, qT_ref, qT_next_ref, k_ref, vT_ref, bvec_ref, g_ref, o_ref,
                      qpad_ref, qpad_next_ref, bias_ref, s_ref, p_ref, m_ref, acc_ref,
                      first_max_ref):
    i = pl.program_id(1)
    t = qT_ref.shape[2]
    n_vec = bvec_ref.shape[2]

    @pl.when(i == 0)
    def _():
        acc_ref[...] = jnp.zeros(acc_ref.shape, _F32)
        p_ref[...] = jnp.zeros(p_ref.shape, _BF16)
        rows = CHUNK
        for hh, (kind, c0) in itertools.product(range(HEADS_PER_STEP),
                                                ((0, 2 * t - 1), (1, t - 1))):
            for r0 in range(0 if kind == 1 else t - MAX_DISTANCE, t, rows):
                x = jnp.broadcast_to(bvec_ref[hh], (rows, n_vec))
                x = pltpu.roll(x, (n_vec - c0 + r0) % n_vec, 1, stride=1, stride_axis=0)[:, :t]
                if kind == 1:
                    key = r0 + lax.broadcasted_iota(jnp.int32, (rows, t), 0)
                    qry = lax.broadcasted_iota(jnp.int32, (rows, t), 1)
                    x = jnp.where(key // CHUNK <= qry // CHUNK, x, NEG_INF)
                bias_ref[hh, kind, r0:r0 + rows, :] = x

    for src_ref, dst_ref in ((qT_ref, qpad_ref), (qT_next_ref, qpad_next_ref)):
        for hh in range(HEADS_PER_STEP):
            qT = src_ref[0, hh * 2 * HEAD_DIM:(hh + 1) * 2 * HEAD_DIM, :]
            row = lax.broadcasted_iota(jnp.int32, qT.shape, 0)
            zero = jnp.zeros_like(qT)
            dst_ref[2 * hh] = jnp.where(row < HEAD_DIM, qT, zero)
            dst_ref[2 * hh + 1] = jnp.where(row >= HEAD_DIM, qT, zero)

    m_ref[...] = jnp.full(m_ref.shape, NEG_INF, _F32)
    streams = range(2 * HEADS_PER_STEP)
    no_rescale = (jnp.ones((1, t), _F32),) * len(streams)

    def k_tile(tile):
        kt = k_ref[tile]
        return [kt[:, hh * 2 * HEAD_DIM:(hh + 1) * 2 * HEAD_DIM] for hh in range(HEADS_PER_STEP)]

    def v_tile(tile):
        vt = vT_ref[tile]
        return [vt[hh * V_AUG_DIM:(hh + 1) * V_AUG_DIM] for hh in range(HEADS_PER_STEP)]

    def scores(tile, q_ref=qpad_ref):
        kt = k_tile(tile)
        tile_max = []
        for c in streams:
            s = jnp.dot(kt[c // 2], q_ref[c], preferred_element_type=_F32)
            s_ref[c] = s
            tile_max.append(jnp.max(s, axis=0, keepdims=True))
        return tuple(tile_max)

    def softmax(tile_max):
        alphas = []
        for c in streams:
            m_old = m_ref[c]
            m_new = jnp.maximum(m_old, tile_max[c])
            p_ref[c] = jnp.exp2(s_ref[c] - m_new).astype(_BF16)
            m_ref[c] = m_new
            alphas.append(jnp.exp2(m_old - m_new))
        return tuple(alphas)

    def accumulate(tile, alphas):
        vt = v_tile(jnp.maximum(tile, 0))
        for c in streams:
            acc_ref[c] = alphas[c] * acc_ref[c] + jnp.dot(vt[c // 2], p_ref[c],
                                                          preferred_element_type=_F32)

    half = t // 2
    top, bot = slice(0, half), slice(half, t)

    def scores_diag(tile):
        kt = k_tile(tile)
        for c in streams:
            s_ref[c, top, :] = jnp.dot(kt[c // 2][top], qpad_ref[c], preferred_element_type=_F32)
            s_ref[c, bot, bot] = jnp.dot(kt[c // 2][bot], qpad_ref[c, :, bot],
                                         preferred_element_type=_F32)

    def softmax_diag():
        alphas = []
        for c in streams:
            s_top = s_ref[c, top, :] + bias_ref[c // 2, 1, top, :]
            s_bot = s_ref[c, bot, bot] + bias_ref[c // 2, 1, bot, bot]
            max_top = jnp.max(s_top, axis=0, keepdims=True)
            max_bot = jnp.max(s_bot, axis=0, keepdims=True)
            s_max = jnp.concatenate(
                [max_top[:, top], jnp.maximum(max_top[:, bot], max_bot)], axis=1)
            m_old = m_ref[c]
            m_new = jnp.maximum(m_old, s_max)
            p_ref[c, top, :] = jnp.exp2(s_top - m_new).astype(_BF16)
            p_ref[c, bot, bot] = jnp.exp2(s_bot - m_new[:, bot]).astype(_BF16)
            m_ref[c] = m_new
            alphas.append(jnp.exp2(m_old - m_new))
        return tuple(alphas)

    def accumulate_diag(tile, alphas):
        vt = v_tile(tile)
        for c in streams:
            acc_ref[c, :, top] = alphas[c][:, top] * acc_ref[c, :, top] + jnp.dot(
                vt[c // 2][:, top], p_ref[c, top, top], preferred_element_type=_F32)
            acc_ref[c, :, bot] = alphas[c][:, bot] * acc_ref[c, :, bot] + jnp.dot(
                vt[c // 2], p_ref[c, :, bot], preferred_element_type=_F32)

    def scores_for_next_step():
        for c, tile_max in enumerate(scores(0, qpad_next_ref)):
            first_max_ref[c] = tile_max

    n_far = jnp.maximum(i - 1, 0)

    @pl.when(i == 0)
    def _():
        for c, tile_max in enumerate(scores(0)):
            first_max_ref[c] = tile_max

    first_max = tuple(first_max_ref[c] for c in streams)

    def far_step(tile, carry):
        pending, tile_max = carry
        accumulate(tile - 1, pending)
        pending = softmax(tile_max)
        return pending, scores(tile + 1)

    pending, prev_max = lax.fori_loop(0, n_far, far_step, (no_rescale, first_max))

    def add_corner_bias(tile_max):
        keys, qrys = slice(t - MAX_DISTANCE, t), slice(0, MAX_DISTANCE)
        lane = lax.broadcasted_iota(jnp.int32, (1, t), 1)
        biased_max = []
        for c in streams:
            blk = s_ref[c, keys, qrys] + bias_ref[c // 2, 0, keys, qrys]
            s_ref[c, keys, qrys] = blk
            col_max = jnp.maximum(jnp.max(blk, axis=0, keepdims=True),
                                  jnp.max(s_ref[c, 0:t - MAX_DISTANCE, qrys], axis=0,
                                          keepdims=True))
            col_max = jnp.concatenate([col_max] * (t // MAX_DISTANCE), axis=1)
            biased_max.append(jnp.where(lane < MAX_DISTANCE, col_max, tile_max[c]))
        return tuple(biased_max)

    @pl.when(i == 0)
    def _():
        alphas = softmax_diag()
        scores_for_next_step()
        accumulate_diag(0, alphas)

    @pl.when(i > 0)
    def _():
        accumulate(i - 2, pending)
        alphas = softmax(add_corner_bias(prev_max))
        scores_diag(i)
        accumulate(i - 1, alphas)
        alphas = softmax_diag()
        scores_for_next_step()
        accumulate_diag(i, alphas)

    lam = lam_ref[0, 0]
    for hh in range(HEADS_PER_STEP):
        a0, a1 = 2 * hh, 2 * hh + 1
        o = (acc_ref[a0, :V_HEAD_DIM] / acc_ref[a0, V_HEAD_DIM:V_HEAD_DIM + 1]
             - lam * (acc_ref[a1, :V_HEAD_DIM] / acc_ref[a1, V_HEAD_DIM:V_HEAD_DIM + 1]))
        ms = jnp.mean(o * o, axis=0, keepdims=True)
        o = o * lax.rsqrt(ms + SUBLN_EPS) * g_ref[...] * (1.0 - LAMBDA_INIT)
        o_ref[hh * V_HEAD_DIM:(hh + 1) * V_HEAD_DIM, :] = o.astype(_BF16)


def _diff_attn(lam, qT, k, vT, bias_vec, subln_g_col):
    n_tiles, _, t = qT.shape
    seq = n_tiles * t
    n_vec = bias_vec.shape[2]
    hp = HEADS_PER_STEP
    return pl.pallas_call(
        _diff_attn_kernel,
        grid=(N_HEADS // HEADS_PER_STEP, n_tiles),
        in_specs=[
            pl.BlockSpec(memory_space=pltpu.SMEM),
            pl.BlockSpec((1, hp * 2 * HEAD_DIM, t), lambda h, i: (i, h, 0)),
            pl.BlockSpec((1, hp * 2 * HEAD_DIM, t),
                         lambda h, i: (jnp.minimum(i + 1, n_tiles - 1), h, 0)),
            pl.BlockSpec((n_tiles, t, hp * 2 * HEAD_DIM), lambda h, i: (0, 0, h)),
            pl.BlockSpec((n_tiles, hp * V_AUG_DIM, t), lambda h, i: (0, h, 0)),
            pl.BlockSpec((hp, 1, n_vec), lambda h, i: (h, 0, 0)),
            pl.BlockSpec((V_HEAD_DIM, 1), lambda h, i: (0, 0)),
        ],
        out_specs=pl.BlockSpec((hp * V_HEAD_DIM, t), lambda h, i: (h, i)),
        out_shape=jax.ShapeDtypeStruct((ATTN_WIDTH, seq), _BF16),
        scratch_shapes=[
            pltpu.VMEM((2 * hp, 2 * HEAD_DIM, t), _BF16),
            pltpu.VMEM((2 * hp, 2 * HEAD_DIM, t), _BF16),
            pltpu.VMEM((hp, 2, t, t), _F32),
            pltpu.VMEM((2 * hp, t, t), _F32),
            pltpu.VMEM((2 * hp, t, t), _BF16),
            pltpu.VMEM((2 * hp, 1, t), _F32),
            pltpu.VMEM((2 * hp, V_AUG_DIM, t), _F32),
            pltpu.VMEM((2 * hp, 1, t), _F32),
        ],
        compiler_params=pltpu.CompilerParams(
            dimension_semantics=("arbitrary", "arbitrary"), vmem_limit_bytes=VMEM_LIMIT_BYTES),
        name="diff_attn",
    )(lam, qT, qT, k, vT, bias_vec, subln_g_col)


def _t5_bucket(rel):
    nb = NUM_BUCKETS // 2
    max_exact = nb // 2
    bucket = (rel > 0).astype(jnp.int32) * nb
    n = jnp.abs(rel)
    n_f = jnp.maximum(n, 1).astype(jnp.float32)
    large = max_exact + (jnp.log(n_f / max_exact) / math.log(MAX_DISTANCE / max_exact)
                         * (nb - max_exact)).astype(jnp.int32)
    large = jnp.minimum(large, nb - 1)
    return bucket + jnp.where(n < max_exact, n, large)


def _near_bias_vector(rel_table, t):
    assert t >= MAX_DISTANCE and t % CHUNK == 0
    rel = jnp.arange(3 * t - 1, dtype=jnp.int32) - (2 * t - 1)
    far = rel_table[_t5_bucket(jnp.int32(-(t + 1)))].astype(_F32)
    vec = (rel_table[_t5_bucket(rel)].astype(_F32) - far) * LOG2_E
    vec = jnp.pad(vec[::-1], ((0, 1), (0, 0)))
    return vec.T.reshape(rel_table.shape[1], 1, 3 * t)


def _mixer_out_kernel(x_ref, onT_ref, ga_ref, gp_ref, w_pa_ref, w_out_ref, g_mlp_ref,
                      w_up_ref, w_down_ref, g_fin_ref, out_ref):
    y_attn = lax.dot_general(onT_ref[...], w_pa_ref[...], (((0,), (0,)), ((), ())),
                             preferred_element_type=_F32)
    merged = ga_ref[...].astype(_F32) * y_attn + gp_ref[...]
    x1 = x_ref[...] + jnp.dot(merged.astype(_BF16), w_out_ref[...], preferred_element_type=_F32)
    h = _rms_norm(x1, g_mlp_ref[...], NORM_EPS).astype(_BF16)
    x2 = x1
    for c in range(0, D_FF, FF_CHUNK):
        up = jnp.dot(h, w_up_ref[:, c:c + FF_CHUNK], preferred_element_type=_F32)
        act = jnp.square(jnp.maximum(up, 0.0)).astype(_BF16)
        x2 = x2 + jnp.dot(act, w_down_ref[c:c + FF_CHUNK, :], preferred_element_type=_F32)
    out_ref[...] = _rms_norm(x2, g_fin_ref[...], NORM_EPS)


def _mixer_out(x2d, on, ga, gp, w_pa, w_out, g_mlp, w_up, w_down, g_fin):
    seq = x2d.shape[0]
    tm = SEQ_TILE
    row_tile = pl.BlockSpec((tm, D_MODEL), lambda i: (i, 0))
    return pl.pallas_call(
        _mixer_out_kernel,
        grid=(seq // tm,),
        in_specs=[
            row_tile, pl.BlockSpec((ATTN_WIDTH, tm), lambda i: (0, i)), row_tile, row_tile,
            _resident((ATTN_WIDTH, D_MODEL)),
            _resident((D_MODEL, D_MODEL)),
            _resident((1, D_MODEL)),
            _resident((D_MODEL, D_FF)),
            _resident((D_FF, D_MODEL)),
            _resident((1, D_MODEL)),
        ],
        out_specs=row_tile,
        out_shape=jax.ShapeDtypeStruct((seq, D_MODEL), _F32),
        compiler_params=pltpu.CompilerParams(
            dimension_semantics=("arbitrary",), vmem_limit_bytes=VMEM_LIMIT_BYTES),
        name="mixer_out",
    )(x2d, on, ga, gp, w_pa, w_out, g_mlp, w_up, w_down, g_fin)


def kernel(x, ln_mix_g, w_in, rel_bias_table, lambda_q1, lambda_k1, lambda_q2, lambda_k2,
           subln_g, w_proj_attn, w_pool_grp, pool_scale, w_proj_pool, w_out,
           ln_mlp_g, w_mlp_up, w_mlp_down, ln_final_g):
    batch, seq, d_model = x.shape
    assert batch == 1 and d_model == D_MODEL and seq % SEQ_TILE == 0
    assert ln_mix_g.shape[0] == 1, "single-layer trunk"
    x2d = x.reshape(seq, d_model)

    qT, k, vT, ga, gp = _mixer_in(
        x2d, ln_mix_g, w_in[0].astype(_BF16), w_pool_grp[0].astype(_BF16), pool_scale,
        w_proj_pool[0].astype(_BF16))

    lam = pl.pallas_call(
        _lambda_kernel, out_shape=jax.ShapeDtypeStruct((1, 1), _F32), name="diff_lambda",
    )(lambda_q1, lambda_k1, lambda_q2, lambda_k2)
    bias_vec = _near_bias_vector(rel_bias_table, SEQ_TILE)
    on = _diff_attn(lam, qT, k, vT, bias_vec, subln_g.reshape(V_HEAD_DIM, 1))

    out = _mixer_out(
        x2d, on, ga, gp, w_proj_attn[0].astype(_BF16), w_out[0].astype(_BF16), ln_mlp_g,
        w_mlp_up[0].astype(_BF16), w_mlp_down[0].astype(_BF16), ln_final_g.reshape(1, d_model))
    return out.reshape(batch, seq, d_model)
```

```python
import itertools
import math

import jax
import jax.numpy as jnp
from jax import lax
from jax.experimental import pallas as pl
from jax.experimental.pallas import tpu as pltpu

D_MODEL = 1024
N_HEADS = 8
HEAD_DIM = 64
V_HEAD_DIM = 2 * HEAD_DIM
QK_WIDTH = N_HEADS * 2 * HEAD_DIM
ATTN_WIDTH = N_HEADS * V_HEAD_DIM
BF16_SUBLANES = 16
V_AUG_DIM = V_HEAD_DIM + BF16_SUBLANES
POOL_WINDOWS = (2, 4, 8, 16)
POOL_GROUP_WIDTH = D_MODEL // len(POOL_WINDOWS)
MAX_WINDOW = max(POOL_WINDOWS)
D_FF = 4 * D_MODEL
CHUNK = 64
NUM_BUCKETS = 32
MAX_DISTANCE = 128
NORM_EPS = 1e-6
SUBLN_EPS = 1e-5
NEG_INF = -1e30
LAMBDA_INIT = 0.8 - 0.6 * math.exp(-0.3 * 0)
LOG2_E = math.log2(math.e)

SEQ_TILE = 512
FF_CHUNK = 1024
HEADS_PER_STEP = 2
VMEM_LIMIT_BYTES = 56 * 1024 * 1024

_BF16 = jnp.bfloat16
_F32 = jnp.float32


def _rms_norm(x, g, eps):
    return x * lax.rsqrt(jnp.mean(x * x, axis=-1, keepdims=True) + eps) * g


def _resident(shape):
    return pl.BlockSpec(shape, lambda *_: (0,) * len(shape), pipeline_mode=pl.Buffered(1))


def _mixer_in_kernel(x_ref, g_ref, w_in_ref, w_grp_ref, pscale_ref, w_pp_ref,
                     qT_ref, k_ref, vT_ref, ga_ref, gp_ref, ubuf_ref):
    i = pl.program_id(0)
    tm = x_ref.shape[0]
    h = _rms_norm(x_ref[...], g_ref[...], NORM_EPS).astype(_BF16)

    def proj(col, width):
        return jnp.dot(h, w_in_ref[:, col:col + width], preferred_element_type=_F32)

    q = proj(0, QK_WIDTH) * (HEAD_DIM ** -0.5 * LOG2_E)
    qT_ref[0] = q.T.astype(_BF16)
    k_ref[0] = proj(QK_WIDTH, QK_WIDTH).astype(_BF16)
    vT = proj(2 * QK_WIDTH, ATTN_WIDTH).T.astype(_BF16)
    ones = jnp.ones((BF16_SUBLANES, tm), _BF16)
    for hd in range(N_HEADS):
        vT_ref[0, hd * V_AUG_DIM:hd * V_AUG_DIM + V_HEAD_DIM, :] = (
            vT[hd * V_HEAD_DIM:(hd + 1) * V_HEAD_DIM, :])
        vT_ref[0, hd * V_AUG_DIM + V_HEAD_DIM:(hd + 1) * V_AUG_DIM, :] = ones

    @pl.when(i == 0)
    def _():
        ubuf_ref[0:MAX_WINDOW, :] = jnp.zeros((MAX_WINDOW, D_MODEL), _F32)

    ubuf_ref[MAX_WINDOW:MAX_WINDOW + tm, :] = proj(2 * QK_WIDTH + ATTN_WIDTH, D_MODEL)
    pos = i * tm + lax.broadcasted_iota(jnp.int32, (tm, 1), 0)
    y_pool = jnp.zeros((tm, D_MODEL), _F32)
    for g, w in enumerate(POOL_WINDOWS):
        cols = slice(g * POOL_GROUP_WIDTH, (g + 1) * POOL_GROUP_WIDTH)
        u = ubuf_ref[MAX_WINDOW:MAX_WINDOW + tm, cols]
        window_sum = u
        for s in range(1, w):
            window_sum = window_sum + ubuf_ref[MAX_WINDOW - s:MAX_WINDOW - s + tm, cols]
        count = jnp.minimum(pos + 1, w).astype(_F32)
        z = window_sum / count - u
        zg = jnp.dot(z.astype(_BF16), w_grp_ref[g], preferred_element_type=_F32)
        zg = zg * pscale_ref[:, cols]
        y_pool = y_pool + jnp.dot(zg.astype(_BF16), w_pp_ref[cols, :],
                                  preferred_element_type=_F32)
    ubuf_ref[0:MAX_WINDOW, :] = ubuf_ref[tm:tm + MAX_WINDOW, :]

    gate_col = 2 * QK_WIDTH + ATTN_WIDTH + D_MODEL
    ga_ref[...] = jax.nn.sigmoid(proj(gate_col, D_MODEL)).astype(_BF16)
    gp_ref[...] = jax.nn.sigmoid(proj(gate_col + D_MODEL, D_MODEL)) * y_pool


def _mixer_in(x2d, ln_g, w_in, w_grp, pool_scale, w_pp):
    seq = x2d.shape[0]
    tm = SEQ_TILE
    n_tiles = seq // tm
    in_width = w_in.shape[1]
    row_tile = lambda i: (i, 0)
    tile3 = lambda i: (i, 0, 0)
    return pl.pallas_call(
        _mixer_in_kernel,
        grid=(n_tiles,),
        in_specs=[
            pl.BlockSpec((tm, D_MODEL), row_tile),
            _resident((1, D_MODEL)),
            _resident((D_MODEL, in_width)),
            _resident(w_grp.shape),
            _resident((1, D_MODEL)),
            _resident((D_MODEL, D_MODEL)),
        ],
        out_specs=[
            pl.BlockSpec((1, QK_WIDTH, tm), tile3),
            pl.BlockSpec((1, tm, QK_WIDTH), tile3),
            pl.BlockSpec((1, N_HEADS * V_AUG_DIM, tm), tile3),
            pl.BlockSpec((tm, D_MODEL), row_tile),
            pl.BlockSpec((tm, D_MODEL), row_tile),
        ],
        out_shape=[
            jax.ShapeDtypeStruct((n_tiles, QK_WIDTH, tm), _BF16),
            jax.ShapeDtypeStruct((n_tiles, tm, QK_WIDTH), _BF16),
            jax.ShapeDtypeStruct((n_tiles, N_HEADS * V_AUG_DIM, tm), _BF16),
            jax.ShapeDtypeStruct((seq, D_MODEL), _BF16),
            jax.ShapeDtypeStruct((seq, D_MODEL), _F32),
        ],
        scratch_shapes=[pltpu.VMEM((tm + MAX_WINDOW, D_MODEL), _F32)],
        compiler_params=pltpu.CompilerParams(
            dimension_semantics=("arbitrary",), vmem_limit_bytes=VMEM_LIMIT_BYTES),
        name="mixer_in",
    )(x2d, ln_g, w_in, w_grp, pool_scale, w_pp)


def _lambda_kernel(lq1_ref, lk1_ref, lq2_ref, lk2_ref, lam_ref):
    a = jnp.sum(lq1_ref[...] * lk1_ref[...], axis=-1, keepdims=True)
    b = jnp.sum(lq2_ref[...] * lk2_ref[...], axis=-1, keepdims=True)
    lam_ref[...] = jnp.exp(a) - jnp.exp(b) + LAMBDA_INIT


def _diff_attn_kernel(lam_ref, qT_ref, qT_next_ref, k_ref, vT_ref, bvec_ref, g_ref, o_ref,
                      qpad_ref, qpad_next_ref, bias_ref, s_ref, p_ref, m_ref, acc_ref,
                      first_max_ref):
    i = pl.program_id(1)
    t = qT_ref.shape[2]
    n_vec = bvec_ref.shape[2]

    @pl.when(i == 0)
    def _():
        acc_ref[...] = jnp.zeros(acc_ref.shape, _F32)
        p_ref[...] = jnp.zeros(p_ref.shape, _BF16)
        rows = CHUNK
        for hh, (kind, c0) in itertools.product(range(HEADS_PER_STEP),
                                                ((0, 2 * t - 1), (1, t - 1))):
            for r0 in range(0 if kind == 1 else t - MAX_DISTANCE, t, rows):
                x = jnp.broadcast_to(bvec_ref[hh], (rows, n_vec))
                x = pltpu.roll(x, (n_vec - c0 + r0) % n_vec, 1, stride=1, stride_axis=0)[:, :t]
                if kind == 1:
                    key = r0 + lax.broadcasted_iota(jnp.int32, (rows, t), 0)
                    qry = lax.broadcasted_iota(jnp.int32, (rows, t), 1)
                    x = jnp.where(key // CHUNK <= qry // CHUNK, x, NEG_INF)
                bias_ref[hh, kind, r0:r0 + rows, :] = x

    for src_ref, dst_ref in ((qT_ref, qpad_ref), (qT_next_ref, qpad_next_ref)):
        for hh in range(HEADS_PER_STEP):
            qT = src_ref[0, hh * 2 * HEAD_DIM:(hh + 1) * 2 * HEAD_DIM, :]
            row = lax.broadcasted_iota(jnp.int32, qT.shape, 0)
            zero = jnp.zeros_like(qT)
            dst_ref[2 * hh] = jnp.where(row < HEAD_DIM, qT, zero)
            dst_ref[2 * hh + 1] = jnp.where(row >= HEAD_DIM, qT, zero)

    m_ref[...] = jnp.full(m_ref.shape, NEG_INF, _F32)
    streams = range(2 * HEADS_PER_STEP)
    no_rescale = (jnp.ones((1, t), _F32),) * len(streams)

    def k_tile(tile):
        kt = k_ref[tile]
        return [kt[:, hh * 2 * HEAD_DIM:(hh + 1) * 2 * HEAD_DIM] for hh in range(HEADS_PER_STEP)]

    def v_tile(tile):
        vt = vT_ref[tile]
        return [vt[hh * V_AUG_DIM:(hh + 1) * V_AUG_DIM] for hh in range(HEADS_PER_STEP)]

    def scores(tile, q_ref=qpad_ref):
        kt = k_tile(tile)
        tile_max = []
        for c in streams:
            s = jnp.dot(kt[c // 2], q_ref[c], preferred_element_type=_F32)
            s_ref[c] = s
            tile_max.append(jnp.max(s, axis=0, keepdims=True))
        return tuple(tile_max)

    def softmax(tile_max):
        alphas = []
        for c in streams:
            m_old = m_ref[c]
            m_new = jnp.maximum(m_old, tile_max[c])
            p_ref[c] = jnp.exp2(s_ref[c] - m_new).astype(_BF16)
            m_ref[c] = m_new
            alphas.append(jnp.exp2(m_old - m_new))
        return tuple(alphas)

    def accumulate(tile, alphas):
        vt = v_tile(jnp.maximum(tile, 0))
        for c in streams:
            acc_ref[c] = alphas[c] * acc_ref[c] + jnp.dot(vt[c // 2], p_ref[c],
                                                          preferred_element_type=_F32)

    half = t // 2
    top, bot = slice(0, half), slice(half, t)

    def scores_diag(tile):
        kt = k_tile(tile)
        for c in streams:
            s_ref[c, top, :] = jnp.dot(kt[c // 2][top], qpad_ref[c], preferred_element_type=_F32)
            s_ref[c, bot, bot] = jnp.dot(kt[c // 2][bot], qpad_ref[c, :, bot],
                                         preferred_element_type=_F32)

    def softmax_diag():
        alphas = []
        for c in streams:
            s_top = s_ref[c, top, :] + bias_ref[c // 2, 1, top, :]
            s_bot = s_ref[c, bot, bot] + bias_ref[c // 2, 1, bot, bot]
            max_top = jnp.max(s_top, axis=0, keepdims=True)
            max_bot = jnp.max(s_bot, axis=0, keepdims=True)
            s_max = jnp.concatenate(
                [max_top[:, top], jnp.maximum(max_top[:, bot], max_bot)], axis=1)
            m_old = m_ref[c]
            m_new = jnp.maximum(m_old, s_max)
            p_ref[c, top, :] = jnp.exp2(s_top - m_new).astype(_BF16)
            p_ref[c, bot, bot] = jnp.exp2(s_bot - m_new[:, bot]).astype(_BF16)
            m_ref[c] = m_new
            alphas.append(jnp.exp2(m_old - m_new))
        return tuple(alphas)

    def accumulate_diag(tile, alphas):
        vt = v_tile(tile)
        for c in streams:
            acc_ref[c, :, top] = alphas[c][:, top] * acc_ref[c, :, top] + jnp.dot(
                vt[c // 2][:, top], p_ref[c, top, top], preferred_element_type=_F32)
            acc_ref[c, :, bot] = alphas[c][:, bot] * acc_ref[c, :, bot] + jnp.dot(
                vt[c // 2], p_ref[c, :, bot], preferred_element_type=_F32)

    def scores_for_next_step():
        for c, tile_max in enumerate(scores(0, qpad_next_ref)):
            first_max_ref[c] = tile_max

    n_far = jnp.maximum(i - 1, 0)

    @pl.when(i == 0)
    def _():
        for c, tile_max in enumerate(scores(0)):
            first_max_ref[c] = tile_max

    first_max = tuple(first_max_ref[c] for c in streams)

    def far_step(tile, carry):
        pending, tile_max = carry
        vts = v_tile(jnp.maximum(tile - 1, 0))
        kts = k_tile(tile + 1)
        new_pending, new_max = list(pending), list(tile_max)
        for hh in range(HEADS_PER_STEP):
            sub = (2 * hh, 2 * hh + 1)
            for c in sub:
                acc_ref[c] = pending[c] * acc_ref[c] + jnp.dot(vts[hh], p_ref[c],
                                                               preferred_element_type=_F32)
            for c in sub:
                m_old = m_ref[c]
                m_new = jnp.maximum(m_old, tile_max[c])
                p_ref[c] = jnp.exp2(s_ref[c] - m_new).astype(_BF16)
                m_ref[c] = m_new
                new_pending[c] = jnp.exp2(m_old - m_new)
            for c in sub:
                sc = jnp.dot(kts[hh], qpad_ref[c], preferred_element_type=_F32)
                s_ref[c] = sc
                new_max[c] = jnp.max(sc, axis=0, keepdims=True)
        return tuple(new_pending), tuple(new_max)

    pending, prev_max = lax.fori_loop(0, n_far, far_step, (no_rescale, first_max))

    def add_corner_bias(tile_max):
        keys, qrys = slice(t - MAX_DISTANCE, t), slice(0, MAX_DISTANCE)
        lane = lax.broadcasted_iota(jnp.int32, (1, t), 1)
        biased_max = []
        for c in streams:
            blk = s_ref[c, keys, qrys] + bias_ref[c // 2, 0, keys, qrys]
            s_ref[c, keys, qrys] = blk
            col_max = jnp.maximum(jnp.max(blk, axis=0, keepdims=True),
                                  jnp.max(s_ref[c, 0:t - MAX_DISTANCE, qrys], axis=0,
                                          keepdims=True))
            col_max = jnp.concatenate([col_max] * (t // MAX_DISTANCE), axis=1)
            biased_max.append(jnp.where(lane < MAX_DISTANCE, col_max, tile_max[c]))
        return tuple(biased_max)

    @pl.when(i == 0)
    def _():
        alphas = softmax_diag()
        scores_for_next_step()
        accumulate_diag(0, alphas)

    @pl.when(i > 0)
    def _():
        accumulate(i - 2, pending)
        alphas = softmax(add_corner_bias(prev_max))
        scores_diag(i)
        accumulate(i - 1, alphas)
        alphas = softmax_diag()
        scores_for_next_step()
        accumulate_diag(i, alphas)

    lam = lam_ref[0, 0]
    for hh in range(HEADS_PER_STEP):
        a0, a1 = 2 * hh, 2 * hh + 1
        o = (acc_ref[a0, :V_HEAD_DIM] / acc_ref[a0, V_HEAD_DIM:V_HEAD_DIM + 1]
             - lam * (acc_ref[a1, :V_HEAD_DIM] / acc_ref[a1, V_HEAD_DIM:V_HEAD_DIM + 1]))
        ms = jnp.mean(o * o, axis=0, keepdims=True)
        o = o * lax.rsqrt(ms + SUBLN_EPS) * g_ref[...] * (1.0 - LAMBDA_INIT)
        o_ref[hh * V_HEAD_DIM:(hh + 1) * V_HEAD_DIM, :] = o.astype(_BF16)


def _diff_attn(lam, qT, k, vT, bias_vec, subln_g_col):
    n_tiles, _, t = qT.shape
    seq = n_tiles * t
    n_vec = bias_vec.shape[2]
    hp = HEADS_PER_STEP
    return pl.pallas_call(
        _diff_attn_kernel,
        grid=(N_HEADS // HEADS_PER_STEP, n_tiles),
        in_specs=[
            pl.BlockSpec(memory_space=pltpu.SMEM),
            pl.BlockSpec((1, hp * 2 * HEAD_DIM, t), lambda h, i: (i, h, 0)),
            pl.BlockSpec((1, hp * 2 * HEAD_DIM, t),
                         lambda h, i: (jnp.minimum(i + 1, n_tiles - 1), h, 0)),
            pl.BlockSpec((n_tiles, t, hp * 2 * HEAD_DIM), lambda h, i: (0, 0, h)),
            pl.BlockSpec((n_tiles, hp * V_AUG_DIM, t), lambda h, i: (0, h, 0)),
            pl.BlockSpec((hp, 1, n_vec), lambda h, i: (h, 0, 0)),
            pl.BlockSpec((V_HEAD_DIM, 1), lambda h, i: (0, 0)),
        ],
        out_specs=pl.BlockSpec((hp * V_HEAD_DIM, t), lambda h, i: (h, i)),
        out_shape=jax.ShapeDtypeStruct((ATTN_WIDTH, seq), _BF16),
        scratch_shapes=[
            pltpu.VMEM((2 * hp, 2 * HEAD_DIM, t), _BF16),
            pltpu.VMEM((2 * hp, 2 * HEAD_DIM, t), _BF16),
            pltpu.VMEM((hp, 2, t, t), _F32),
            pltpu.VMEM((2 * hp, t, t), _F32),
            pltpu.VMEM((2 * hp, t, t), _BF16),
            pltpu.VMEM((2 * hp, 1, t), _F32),
            pltpu.VMEM((2 * hp, V_AUG_DIM, t), _F32),
            pltpu.VMEM((2 * hp, 1, t), _F32),
        ],
        compiler_params=pltpu.CompilerParams(
            dimension_semantics=("arbitrary", "arbitrary"), vmem_limit_bytes=VMEM_LIMIT_BYTES),
        name="diff_attn",
    )(lam, qT, qT, k, vT, bias_vec, subln_g_col)


def _t5_bucket(rel):
    nb = NUM_BUCKETS // 2
    max_exact = nb // 2
    bucket = (rel > 0).astype(jnp.int32) * nb
    n = jnp.abs(rel)
    n_f = jnp.maximum(n, 1).astype(jnp.float32)
    large = max_exact + (jnp.log(n_f / max_exact) / math.log(MAX_DISTANCE / max_exact)
                         * (nb - max_exact)).astype(jnp.int32)
    large = jnp.minimum(large, nb - 1)
    return bucket + jnp.where(n < max_exact, n, large)


def _near_bias_vector(rel_table, t):
    assert t >= MAX_DISTANCE and t % CHUNK == 0
    rel = jnp.arange(3 * t - 1, dtype=jnp.int32) - (2 * t - 1)
    far = rel_table[_t5_bucket(jnp.int32(-(t + 1)))].astype(_F32)
    vec = (rel_table[_t5_bucket(rel)].astype(_F32) - far) * LOG2_E
    vec = jnp.pad(vec[::-1], ((0, 1), (0, 0)))
    return vec.T.reshape(rel_table.shape[1], 1, 3 * t)


def _mixer_out_kernel(x_ref, onT_ref, ga_ref, gp_ref, w_pa_ref, w_out_ref, g_mlp_ref,
                      w_up_ref, w_down_ref, g_fin_ref, out_ref):
    y_attn = lax.dot_general(onT_ref[...], w_pa_ref[...], (((0,), (0,)), ((), ())),
                             preferred_element_type=_F32)
    merged = ga_ref[...].astype(_F32) * y_attn + gp_ref[...]
    x1 = x_ref[...] + jnp.dot(merged.astype(_BF16), w_out_ref[...], preferred_element_type=_F32)
    h = _rms_norm(x1, g_mlp_ref[...], NORM_EPS).astype(_BF16)
    x2 = x1
    for c in range(0, D_FF, FF_CHUNK):
        up = jnp.dot(h, w_up_ref[:, c:c + FF_CHUNK], preferred_element_type=_F32)
        act = jnp.square(jnp.maximum(up, 0.0)).astype(_BF16)
        x2 = x2 + jnp.dot(act, w_down_ref[c:c + FF_CHUNK, :], preferred_element_type=_F32)
    out_ref[...] = _rms_norm(x2, g_fin_ref[...], NORM_EPS)


def _mixer_out(x2d, on, ga, gp, w_pa, w_out, g_mlp, w_up, w_down, g_fin):
    seq = x2d.shape[0]
    tm = SEQ_TILE
    row_tile = pl.BlockSpec((tm, D_MODEL), lambda i: (i, 0))
    return pl.pallas_call(
        _mixer_out_kernel,
        grid=(seq // tm,),
        in_specs=[
            row_tile, pl.BlockSpec((ATTN_WIDTH, tm), lambda i: (0, i)), row_tile, row_tile,
            _resident((ATTN_WIDTH, D_MODEL)),
            _resident((D_MODEL, D_MODEL)),
            _resident((1, D_MODEL)),
            _resident((D_MODEL, D_FF)),
            _resident((D_FF, D_MODEL)),
            _resident((1, D_MODEL)),
        ],
        out_specs=row_tile,
        out_shape=jax.ShapeDtypeStruct((seq, D_MODEL), _F32),
        compiler_params=pltpu.CompilerParams(
            dimension_semantics=("arbitrary",), vmem_limit_bytes=VMEM_LIMIT_BYTES),
        name="mixer_out",
    )(x2d, on, ga, gp, w_pa, w_out, g_mlp, w_up, w_down, g_fin)


def kernel(x, ln_mix_g, w_in, rel_bias_table, lambda_q1, lambda_k1, lambda_q2, lambda_k2,
           subln_g, w_proj_attn, w_pool_grp, pool_scale, w_proj_pool, w_out,
           ln_mlp_g, w_mlp_up, w_mlp_down, ln_final_g):
    batch, seq, d_model = x.shape
    assert batch == 1 and d_model == D_MODEL and seq % SEQ_TILE == 0
    assert ln_mix_g.shape[0] == 1, "single-layer trunk"
    x2d = x.reshape(seq, d_model)

    qT, k, vT, ga, gp = _mixer_in(
        x2d, ln_mix_g, w_in[0].astype(_BF16), w_pool_grp[0].astype(_BF16), pool_scale,
        w_proj_pool[0].astype(_BF16))

    lam = pl.pallas_call(
        _lambda_kernel, out_shape=jax.ShapeDtypeStruct((1, 1), _F32), name="diff_lambda",
    )(lambda_q1, lambda_k1, lambda_q2, lambda_k2)
    bias_vec = _near_bias_vector(rel_bias_table, SEQ_TILE)
    on = _diff_attn(lam, qT, k, vT, bias_vec, subln_g.reshape(V_HEAD_DIM, 1))

    out = _mixer_out(
        x2d, on, ga, gp, w_proj_attn[0].astype(_BF16), w_out[0].astype(_BF16), ln_mlp_g,
        w_mlp_up[0].astype(_BF16), w_mlp_down[0].astype(_BF16), ln_final_g.reshape(1, d_model))
    return out.reshape(batch, seq, d_model)
```
